```python
import jax, jax.numpy as jnp
from jax import lax
import numpy as np

D_MODEL = 1024
BATCH = 2
SEQ = 8192
DEPTH = 4

DN_HEADS = 8
DN_DK = 128
DN_DV = 128
DN_CONV = 4
DN_CHUNK = 64
SW_Q_HEADS = 16
SW_KV_HEADS = 2
SW_HEAD_DIM = 64
SW_WINDOW = 128
SW_BLOCK = 128
ROPE_THETA = 500000.0
ROT_DIM = SW_HEAD_DIM // 4
D_FF = 4 * D_MODEL
EPS = 1e-6

DN_QK_W = DN_HEADS * DN_DK
DN_V_W = DN_HEADS * DN_DV
SW_Q_W = SW_Q_HEADS * SW_HEAD_DIM
SW_KV_W = SW_KV_HEADS * SW_HEAD_DIM
IN_SPLITS = [DN_QK_W, DN_QK_W, DN_V_W, DN_V_W, DN_HEADS, DN_HEADS,
             SW_Q_W, SW_KV_W, SW_KV_W, D_MODEL, D_MODEL]
D_IN = sum(IN_SPLITS)
IN_OFFSETS = np.cumsum(IN_SPLITS)[:-1].tolist()

kernel_name = 'hybrid_gdn_swa_sink_parallel_block'


def rmsnorm(x, g):
    xf = x.astype(jnp.float32)
    y = xf * lax.rsqrt(jnp.mean(xf * xf, axis=-1, keepdims=True) + EPS)
    return (y * g.astype(jnp.float32)).astype(x.dtype)


def l2norm(t):
    tf = t.astype(jnp.float32)
    return tf * lax.rsqrt(jnp.sum(tf * tf, axis=-1, keepdims=True) + EPS)


def causal_conv_silu(x, w):
    S = x.shape[1]
    K = w.shape[0]
    xp = jnp.pad(x, ((0, 0), (K - 1, 0), (0, 0)))
    y = sum(xp[:, j:j + S] * w[j] for j in range(K))
    return jax.nn.silu(y)


def gated_delta_rule(q, k, v, g, beta):
    B, S, H, dk = q.shape
    dv = v.shape[-1]
    C = DN_CHUNK
    N = S // C

    def chunks(t):
        return t.reshape(B, N, C, H, -1).transpose(0, 3, 1, 2, 4)

    q, k, v = chunks(q), chunks(k), chunks(v)
    g = g.reshape(B, N, C, H).transpose(0, 3, 1, 2)
    beta = beta.reshape(B, N, C, H).transpose(0, 3, 1, 2)
    g = jnp.cumsum(g, axis=-1)

    idx = jnp.arange(C)
    causal = idx[:, None] >= idx[None, :]
    strict = idx[:, None] > idx[None, :]
    diff = g[..., :, None] - g[..., None, :]
    decay = jnp.where(causal, jnp.exp(jnp.where(causal, diff, 0.0)), 0.0)

    kb = k * beta[..., None]
    L = jnp.where(strict, jnp.einsum('bhnid,bhnjd->bhnij', kb, k) * decay, 0.0)
    u = lax.linalg.triangular_solve(L, v * beta[..., None], left_side=True,
                                    lower=True, unit_diagonal=True)
    w = lax.linalg.triangular_solve(L, kb * jnp.exp(g)[..., None], left_side=True,
                                    lower=True, unit_diagonal=True)
    a_intra = jnp.einsum('bhnid,bhnjd->bhnij', q, k) * decay
    q_dec = q * jnp.exp(g)[..., None]
    g_last = g[..., -1]
    k_dec = k * jnp.exp(g_last[..., None] - g)[..., None]

    def to_front(t):
        return jnp.moveaxis(t, 2, 0)

    xs = (to_front(q_dec), to_front(k_dec), to_front(u), to_front(w),
          to_front(a_intra), jnp.moveaxis(g_last, 2, 0))

    def step(state, inp):
        qd, kd, u_c, w_c, a_c, gl = inp
        v_new = u_c - jnp.einsum('bhcd,bhde->bhce', w_c, state)
        o = (jnp.einsum('bhcd,bhde->bhce', qd, state)
             + jnp.einsum('bhij,bhje->bhie', a_c, v_new))
        state = state * jnp.exp(gl)[..., None, None] + jnp.einsum('bhcd,bhce->bhde', kd, v_new)
        return state, o

    state0 = jnp.zeros((B, H, dk, dv), jnp.float32)
    _, o = lax.scan(step, state0, xs)
    return o.transpose(1, 0, 3, 2, 4).reshape(B, S, H, dv)


def deltanet_branch(q_in, k_in, v_in, z, b_in, a_in, conv_w, a_log, dt_bias, norm_g):
    B, S, _ = q_in.shape
    qkv = causal_conv_silu(jnp.concatenate([q_in, k_in, v_in], axis=-1), conv_w)
    q, k, v = jnp.split(qkv, [DN_QK_W, 2 * DN_QK_W], axis=-1)
    q = l2norm(q.reshape(B, S, DN_HEADS, DN_DK)) * (DN_DK ** -0.5)
    k = l2norm(k.reshape(B, S, DN_HEADS, DN_DK))
    v = v.reshape(B, S, DN_HEADS, DN_DV).astype(jnp.float32)
    beta = jax.nn.sigmoid(b_in.astype(jnp.float32))
    g = -jnp.exp(a_log.astype(jnp.float32)) * jax.nn.softplus(
        a_in.astype(jnp.float32) + dt_bias.astype(jnp.float32))
    o = gated_delta_rule(q, k, v, g, beta)
    o = rmsnorm(o, norm_g) * jax.nn.silu(z.reshape(B, S, DN_HEADS, DN_DV).astype(jnp.float32))
    return o.reshape(B, S, DN_V_W).astype(q_in.dtype)


def partial_rope(x, positions):
    half = ROT_DIM // 2
    inv_freq = ROPE_THETA ** (-jnp.arange(half, dtype=jnp.float32) * (2.0 / ROT_DIM))
    ang = positions.astype(jnp.float32)[..., None] * inv_freq
    cos = jnp.cos(ang)[:, :, None, :]
    sin = jnp.sin(ang)[:, :, None, :]
    xr = x[..., :ROT_DIM].astype(jnp.float32)
    x1, x2 = xr[..., :half], xr[..., half:]
    rot = jnp.concatenate([x1 * cos - x2 * sin, x2 * cos + x1 * sin], axis=-1)
    return jnp.concatenate([rot.astype(x.dtype), x[..., ROT_DIM:]], axis=-1)


def swa_sink_branch(q_in, k_in, v_in, positions, sinks):
    B, S, _ = q_in.shape
    G = SW_Q_HEADS // SW_KV_HEADS
    nb = S // SW_BLOCK
    q = partial_rope(q_in.reshape(B, S, SW_Q_HEADS, SW_HEAD_DIM), positions)
    k = partial_rope(k_in.reshape(B, S, SW_KV_HEADS, SW_HEAD_DIM), positions)
    v = v_in.reshape(B, S, SW_KV_HEADS, SW_HEAD_DIM)

    qb = q.reshape(B, nb, SW_BLOCK, SW_KV_HEADS, G, SW_HEAD_DIM).astype(jnp.float32)

    def band(t):
        tp = jnp.pad(t, ((0, 0), (SW_BLOCK, 0), (0, 0), (0, 0)))
        tb = tp.reshape(B, nb + 1, SW_BLOCK, SW_KV_HEADS, SW_HEAD_DIM)
        return jnp.concatenate([tb[:, :-1], tb[:, 1:]], axis=2)

    kw = band(k).astype(jnp.float32)
    vw = band(v)
    scores = jnp.einsum('bnqhgd,bnkhd->bnhgqk', qb, kw) * (SW_HEAD_DIM ** -0.5)

    qi = jnp.arange(SW_BLOCK)[:, None] + SW_BLOCK
    ki = jnp.arange(2 * SW_BLOCK)[None, :]
    off = qi - ki
    in_band = (off >= 0) & (off < SW_WINDOW)
    blk = jnp.arange(nb)[:, None, None]
    valid = (blk * SW_BLOCK + ki[None] - SW_BLOCK) >= 0
    mask = (in_band[None] & valid)[None, :, None, None]
    scores = jnp.where(mask, scores, -jnp.inf)

    sink = sinks.astype(jnp.float32).reshape(SW_KV_HEADS, G)[None, None, :, :, None, None]
    m = jnp.maximum(jnp.max(scores, axis=-1, keepdims=True), sink)
    p = jnp.exp(scores - m)
    probs = p / (jnp.sum(p, axis=-1, keepdims=True) + jnp.exp(sink - m))
    o = jnp.einsum('bnhgqk,bnkhd->bnqhgd', probs.astype(vw.dtype), vw)
    return o.reshape(B, S, SW_Q_W)


def hybrid_layer(x, positions, pre_mix_g, w_in, dn_conv_w, dn_a_log, dn_dt_bias, dn_norm_g,
                 sw_sinks, w_up_dn, w_up_sw, w_o, post_mix_g, pre_mlp_g, w_ff1, w_ff2,
                 post_mlp_g):
    h = rmsnorm(x, pre_mix_g)
    proj = h @ w_in
    (dn_q, dn_k, dn_v, dn_z, dn_b, dn_a, sw_q, sw_k, sw_v,
     gate_a, gate_b) = jnp.split(proj, IN_OFFSETS, axis=-1)
    y_a = deltanet_branch(dn_q, dn_k, dn_v, dn_z, dn_b, dn_a, dn_conv_w, dn_a_log,
                          dn_dt_bias, dn_norm_g) @ w_up_dn
    y_b = swa_sink_branch(sw_q, sw_k, sw_v, positions, sw_sinks) @ w_up_sw
    mix = (jax.nn.sigmoid(gate_a) * y_a + jax.nn.sigmoid(gate_b) * y_b) @ w_o
    x = x + rmsnorm(mix, post_mix_g)

    h2 = rmsnorm(x, pre_mlp_g)
    ff = jnp.square(jax.nn.relu(h2 @ w_ff1)) @ w_ff2
    return x + rmsnorm(ff, post_mlp_g)


def setup_inputs(seed: int = 0) -> dict:
    key = jax.random.key(seed)
    ks = jax.random.split(key, 20)
    f32 = jnp.float32

    def nrm(k, shape, scale):
        return jax.random.normal(k, shape, f32) * scale

    def gain(k, shape):
        return 1.0 + 0.02 * jax.random.normal(k, shape, f32)

    x = jax.random.normal(ks[0], (BATCH, SEQ, D_MODEL), f32)
    positions = jnp.broadcast_to(jnp.arange(SEQ, dtype=jnp.int32), (BATCH, SEQ))
    dt = jnp.exp(jax.random.uniform(ks[5], (DEPTH, DN_HEADS), f32,
                                    np.log(1e-3), np.log(1e-1)))
    return {
        'x': x,
        'positions': positions,
        'pre_mix_g': gain(ks[1], (DEPTH, D_MODEL)),
        'w_in': nrm(ks[2], (DEPTH, D_MODEL, D_IN), D_MODEL ** -0.5),
        'dn_conv_w': nrm(ks[3], (DEPTH, DN_CONV, 2 * DN_QK_W + DN_V_W), DN_CONV ** -0.5),
        'dn_a_log': jnp.log(jax.random.uniform(ks[4], (DEPTH, DN_HEADS), f32, 1.0, 16.0)),
        'dn_dt_bias': dt + jnp.log(-jnp.expm1(-dt)),
        'dn_norm_g': gain(ks[6], (DEPTH, DN_DV)),
        'sw_sinks': nrm(ks[7], (DEPTH, SW_Q_HEADS), 0.5),
        'w_up_dn': nrm(ks[8], (DEPTH, DN_V_W, D_MODEL), DN_V_W ** -0.5),
        'w_up_sw': nrm(ks[9], (DEPTH, SW_Q_W, D_MODEL), SW_Q_W ** -0.5),
        'w_o': nrm(ks[10], (DEPTH, D_MODEL, D_MODEL), D_MODEL ** -0.5),
        'post_mix_g': gain(ks[11], (DEPTH, D_MODEL)),
        'pre_mlp_g': gain(ks[12], (DEPTH, D_MODEL)),
        'w_ff1': nrm(ks[13], (DEPTH, D_MODEL, D_FF), D_MODEL ** -0.5),
        'w_ff2': nrm(ks[14], (DEPTH, D_FF, D_MODEL), D_FF ** -0.5),
        'post_mlp_g': gain(ks[15], (DEPTH, D_MODEL)),
    }


def reference(x, positions, pre_mix_g, w_in, dn_conv_w, dn_a_log, dn_dt_bias, dn_norm_g,
              sw_sinks, w_up_dn, w_up_sw, w_o, post_mix_g, pre_mlp_g, w_ff1, w_ff2,
              post_mlp_g):
    for l in range(DEPTH):
        x = hybrid_layer(x, positions, pre_mix_g[l], w_in[l], dn_conv_w[l], dn_a_log[l],
                         dn_dt_bias[l], dn_norm_g[l], sw_sinks[l], w_up_dn[l], w_up_sw[l],
                         w_o[l], post_mix_g[l], pre_mlp_g[l], w_ff1[l], w_ff2[l],
                         post_mlp_g[l])
    return x
```

```python
import functools

import numpy as np
import jax
import jax.numpy as jnp
from jax import lax
from jax.experimental import pallas as pl
from jax.experimental.pallas import tpu as pltpu

F32 = jnp.float32
BF16 = jnp.bfloat16

D_MODEL = 1024
DN_HEADS = 8
DN_DK = 128
DN_DV = 128
DN_CONV = 4
SW_Q_HEADS = 16
SW_KV_HEADS = 2
SW_HEAD_DIM = 64
SW_BLOCK = 128
ROPE_THETA = 500000.0
ROT_DIM = SW_HEAD_DIM // 4
D_FF = 4 * D_MODEL
EPS = 1e-6

DN_W = DN_HEADS * DN_DK
SW_Q_W = SW_Q_HEADS * SW_HEAD_DIM
SW_KV_W = SW_KV_HEADS * SW_HEAD_DIM
LANES = 128

DN_COLS = 4 * DN_W
SW_COLS = SW_Q_W + 2 * SW_KV_W
GATE_COLS = 2 * D_MODEL
BA_COLS = LANES
PACKED_COLS = DN_COLS + SW_COLS + GATE_COLS + BA_COLS

DN_CHUNK = 128
VMEM_LIMIT = 56 * 1024 * 1024


def _bdot(a, b):
    return jnp.dot(a.astype(BF16), b.astype(BF16), preferred_element_type=F32)


def _bdot_nt(a, b):
    return lax.dot_general(a.astype(BF16), b.astype(BF16), (((1,), (1,)), ((), ())),
                           preferred_element_type=F32)


def _bdot_tn(a, b):
    return lax.dot_general(a.astype(BF16), b.astype(BF16), (((0,), (0,)), ((), ())),
                           preferred_element_type=F32)


def _rms(x, g):
    return x * lax.rsqrt(jnp.mean(x * x, axis=-1, keepdims=True) + EPS) * g


def _sigmoid(x):
    return 1.0 / (1.0 + jnp.exp(-x))


def _inproj_kernel(x_ref, g_ref, w_ref, dn_ref, sw_ref, gate_ref, ba_ref):
    h = _rms(x_ref[...], g_ref[...]).astype(BF16)
    off = 0
    for ref, width in ((dn_ref, DN_COLS), (sw_ref, SW_COLS), (gate_ref, GATE_COLS),
                       (ba_ref, BA_COLS)):
        for c in range(0, width, 512):
            w = min(512, width - c)
            ref[:, c:c + w] = jnp.dot(h, w_ref[:, off + c:off + c + w],
                                      preferred_element_type=F32).astype(ref.dtype)
        off += width


def _inproj(x, g, w, tm=512):
    t = x.shape[0]
    row = lambda i: (i, 0)
    const = lambda i: (0, 0)
    return pl.pallas_call(
        _inproj_kernel,
        grid=(t // tm,),
        in_specs=[pl.BlockSpec((tm, D_MODEL), row),
                  pl.BlockSpec((1, D_MODEL), const),
                  pl.BlockSpec((D_MODEL, PACKED_COLS), const, pipeline_mode=pl.Buffered(1))],
        out_specs=[pl.BlockSpec((tm, DN_COLS), row), pl.BlockSpec((tm, SW_COLS), row),
                   pl.BlockSpec((tm, GATE_COLS), row), pl.BlockSpec((tm, BA_COLS), row)],
        out_shape=[jax.ShapeDtypeStruct((t, DN_COLS), BF16),
                   jax.ShapeDtypeStruct((t, SW_COLS), BF16),
                   jax.ShapeDtypeStruct((t, GATE_COLS), BF16),
                   jax.ShapeDtypeStruct((t, BA_COLS), F32)],
        compiler_params=pltpu.CompilerParams(dimension_semantics=("arbitrary",),
                                             vmem_limit_bytes=VMEM_LIMIT),
        name="inproj",
    )(x, g, w)


def _deltanet_kernel(q_ref, k_ref, v_ref, z_ref, ba_ref, cwq_ref, cwk_ref, cwv_ref,
                     alog_ref, dtb_ref, ng_ref, o_ref, state_ref, tail_ref, buf_ref):
    C = DN_CHUNK
    h = pl.program_id(1)
    n = pl.program_id(2)

    @pl.when(n == 0)
    def _():
        state_ref[...] = jnp.zeros_like(state_ref)
        tail_ref[...] = jnp.zeros_like(tail_ref)

    def conv_silu(x_ref, w_ref, idx):
        x = x_ref[...].astype(F32)
        buf_ref[idx, 0:8, :] = tail_ref[idx]
        buf_ref[idx, 8:8 + C, :] = x
        w = w_ref[...]
        y = x * w[3:4, :]
        for j in range(DN_CONV - 1):
            y = y + buf_ref[idx, 5 + j:5 + j + C, :] * w[j:j + 1, :]
        tail_ref[idx] = x[C - 8:C, :]
        return y * _sigmoid(y)

    def l2n(t):
        return t * lax.rsqrt(jnp.sum(t * t, axis=-1, keepdims=True) + EPS)

    q = l2n(conv_silu(q_ref, cwq_ref, 0)) * (DN_DK ** -0.5)
    k = l2n(conv_silu(k_ref, cwk_ref, 1))
    v = conv_silu(v_ref, cwv_ref, 2)

    ba = ba_ref[...]
    lane = lax.broadcasted_iota(jnp.int32, (C, LANES), 1)
    row = lax.broadcasted_iota(jnp.int32, (C, LANES), 0)
    beta_all = _sigmoid(ba)
    xg = ba + dtb_ref[...]
    softplus = jnp.maximum(xg, 0.0) + jnp.log(1.0 + jnp.exp(-jnp.abs(xg)))
    g_all = -jnp.exp(alog_ref[...]) * softplus
    beta = jnp.sum(jnp.where(lane == h, beta_all, 0.0), axis=-1, keepdims=True)
    g_col = jnp.sum(jnp.where(lane == h + DN_HEADS, g_all, 0.0), axis=-1, keepdims=True)

    gb = jnp.broadcast_to(g_col, (C, LANES))
    g_hi = gb.astype(BF16)
    r1 = gb - g_hi.astype(F32)
    g_mid = r1.astype(BF16)
    g_lo = (r1 - g_mid.astype(F32)).astype(BF16)
    tril = jnp.where(row >= lane, 1.0, 0.0).astype(BF16)
    gc = (jnp.dot(tril, g_hi, preferred_element_type=F32)
          + jnp.dot(tril, g_mid, preferred_element_type=F32)
          + jnp.dot(tril, g_lo, preferred_element_type=F32))
    gc_t = gc.T
    g_last = gc[C - 1:C, :]

    causal = row >= lane
    strict = row > lane
    decay = jnp.where(causal, jnp.exp(jnp.where(causal, gc - gc_t, 0.0)), 0.0)
    eg = jnp.exp(gc)

    kb = k * beta
    a_mat = jnp.where(strict, _bdot_nt(kb, k) * decay, 0.0)
    eye = jnp.where(row == lane, 1.0, 0.0)
    sib = row ^ lane
    t_mat = eye - jnp.where(sib < 2, a_mat, 0.0)
    s = 2
    while s < C:
        a_off = jnp.where((sib >= s) & (sib < 2 * s), a_mat, 0.0)
        t_mat = t_mat - _bdot(t_mat, _bdot(a_off, t_mat))
        s *= 2

    uw = _bdot(t_mat, jnp.concatenate([v * beta, kb * eg], axis=1))
    u = uw[:, :DN_DV]
    w = uw[:, DN_DV:]
    a_intra = jnp.where(causal, _bdot_nt(q, k) * decay, 0.0)
    state = state_ref[...]
    ws = _bdot(jnp.concatenate([w, q * eg], axis=0), state)
    v_new = u - ws[:C]
    o = ws[C:] + _bdot(a_intra, v_new)
    k_dec = k * jnp.exp(g_last - gc)
    state_ref[...] = state * jnp.exp(g_last) + _bdot_tn(k_dec, v_new)

    z = z_ref[...].astype(F32)
    o_ref[...] = (_rms(o, ng_ref[...]) * (z * _sigmoid(z))).astype(o_ref.dtype)


def _deltanet(dn, ba, conv_w, alog_row, dtb_row, norm_g, batch, seq):
    t = dn.shape[0]
    C = DN_CHUNK
    nc = seq // C
    H = DN_HEADS

    def col(group):
        return lambda b, h, n: (b * nc + n, group * H + h)

    def wcol(group):
        return lambda b, h, n: (0, group * H + h)

    const = lambda b, h, n: (0, 0)
    return pl.pallas_call(
        _deltanet_kernel,
        grid=(batch, H, nc),
        in_specs=[pl.BlockSpec((C, LANES), col(0)), pl.BlockSpec((C, LANES), col(1)),
                  pl.BlockSpec((C, LANES), col(2)), pl.BlockSpec((C, LANES), col(3)),
                  pl.BlockSpec((C, LANES), lambda b, h, n: (b * nc + n, 0)),
                  pl.BlockSpec((DN_CONV, LANES), wcol(0)),
                  pl.BlockSpec((DN_CONV, LANES), wcol(1)),
                  pl.BlockSpec((DN_CONV, LANES), wcol(2)),
                  pl.BlockSpec((1, LANES), const), pl.BlockSpec((1, LANES), const),
                  pl.BlockSpec((1, LANES), const)],
        out_specs=pl.BlockSpec((C, LANES), lambda b, h, n: (b * nc + n, h)),
        out_shape=jax.ShapeDtypeStruct((t, DN_W), BF16),
        scratch_shapes=[pltpu.VMEM((DN_DK, DN_DV), F32),
                        pltpu.VMEM((3, 8, LANES), F32),
                        pltpu.VMEM((3, 8 + C, LANES), F32)],
        compiler_params=pltpu.CompilerParams(
            dimension_semantics=("arbitrary", "arbitrary", "arbitrary"),
            vmem_limit_bytes=VMEM_LIMIT),
        name="deltanet",
    )(dn, dn, dn, dn, ba, conv_w, conv_w, conv_w, alog_row, dtb_row, norm_g)


def _rope_table_kernel(pos_ref, freq_ref, mcos_ref, msin1_ref, msin2_ref,
                       c_ref, s1_ref, s2_ref):
    ang = pos_ref[...].astype(F32) * freq_ref[...]
    cos = jnp.cos(ang)
    sin = jnp.sin(ang)
    c_ref[...] = jnp.where(mcos_ref[...] > 0.5, cos, 1.0)
    s1_ref[...] = jnp.where(msin1_ref[...] > 0.5, -sin, 0.0)
    s2_ref[...] = jnp.where(msin2_ref[...] > 0.5, sin, 0.0)


def _rope_tables(positions, tm=1024):
    t = positions.size
    half = ROT_DIM // 2
    lane = np.arange(LANES)
    within = lane % SW_HEAD_DIM
    inv_freq = ROPE_THETA ** (-np.arange(half, dtype=np.float32) * (2.0 / ROT_DIM))
    freq = np.where(within < ROT_DIM, inv_freq[within % half], 0.0).astype(np.float32)
    mcos = (within < ROT_DIM).astype(np.float32)
    msin1 = (within < half).astype(np.float32)
    msin2 = ((within >= half) & (within < ROT_DIM)).astype(np.float32)
    rows = [jnp.asarray(a.reshape(1, LANES)) for a in (freq, mcos, msin1, msin2)]
    const = lambda i: (0, 0)
    row = lambda i: (i, 0)
    out = jax.ShapeDtypeStruct((t, LANES), F32)
    return pl.pallas_call(
        _rope_table_kernel,
        grid=(t // tm,),
        in_specs=[pl.BlockSpec((tm, 1), row)] + [pl.BlockSpec((1, LANES), const)] * 4,
        out_specs=[pl.BlockSpec((tm, LANES), row)] * 3,
        out_shape=[out, out, out],
        name="rope_tables",
    )(positions.reshape(t, 1), *rows)


def _swa_kernel(sink_ref, q_ref, kv_ref, c_ref, s1_ref, s2_ref, o_ref, k_scr, v_scr):
    Q = SW_BLOCK
    n = pl.program_id(1)
    half = SW_HEAD_DIM
    cos = c_ref[...]
    sin1 = s1_ref[...]
    sin2 = s2_ref[...]
    lane = lax.broadcasted_iota(jnp.int32, (Q, LANES), 1)
    low = lane < half

    def rope(x):
        return x * cos + pltpu.roll(x, LANES - ROT_DIM // 2, 1) * sin1 \
            + pltpu.roll(x, ROT_DIM // 2, 1) * sin2

    @pl.when(n == 0)
    def _():
        k_scr[...] = jnp.zeros_like(k_scr)
        v_scr[...] = jnp.zeros_like(v_scr)

    @pl.when(n > 0)
    def _():
        for j in range(SW_KV_HEADS):
            k_scr[j, 0:Q, :] = k_scr[j, Q:2 * Q, :]
            v_scr[2 * j, 0:Q, :] = v_scr[2 * j, Q:2 * Q, :]
            v_scr[2 * j + 1, 0:Q, :] = v_scr[2 * j + 1, Q:2 * Q, :]

    kv = kv_ref[...].astype(F32)
    kr = rope(kv[:, :LANES])
    kr_sw = pltpu.roll(kr, half, 1)
    vt = kv[:, LANES:]
    vt_sw = pltpu.roll(vt, half, 1)
    k_scr[0, Q:2 * Q, :] = jnp.where(low, kr, kr_sw).astype(BF16)
    k_scr[1, Q:2 * Q, :] = jnp.where(low, kr_sw, kr).astype(BF16)
    v_scr[0, Q:2 * Q, :] = jnp.where(low, vt, 0.0).astype(BF16)
    v_scr[1, Q:2 * Q, :] = jnp.where(low, 0.0, vt_sw).astype(BF16)
    v_scr[2, Q:2 * Q, :] = jnp.where(low, vt_sw, 0.0).astype(BF16)
    v_scr[3, Q:2 * Q, :] = jnp.where(low, 0.0, vt).astype(BF16)

    qi = lax.broadcasted_iota(jnp.int32, (Q, 2 * Q), 0)
    ki = lax.broadcasted_iota(jnp.int32, (Q, 2 * Q), 1)
    valid = (ki > qi) & (ki <= qi + Q) & ((ki >= Q) | (n > 0))

    def attend(qh, keys, vals, sink):
        s = jnp.where(valid, _bdot_nt(qh, keys), -jnp.inf)
        m = jnp.maximum(jnp.max(s, axis=-1, keepdims=True), sink)
        p = jnp.exp(s - m)
        denom = jnp.sum(p, axis=-1, keepdims=True) + jnp.exp(sink - m)
        return _bdot(p, vals) * (1.0 / denom)

    pairs_per_kv = SW_Q_HEADS // SW_KV_HEADS // 2
    for pair in range(SW_Q_HEADS // 2):
        j = pair // pairs_per_kv
        qp = rope(q_ref[:, pair * LANES:(pair + 1) * LANES].astype(F32)) * (SW_HEAD_DIM ** -0.5)
        keys = k_scr[j]
        o_lo = attend(jnp.where(low, qp, 0.0), keys, v_scr[2 * j], sink_ref[2 * pair])
        o_hi = attend(jnp.where(low, 0.0, qp), keys, v_scr[2 * j + 1], sink_ref[2 * pair + 1])
        o_ref[:, pair * LANES:(pair + 1) * LANES] = (o_lo + o_hi).astype(o_ref.dtype)


def _swa(sw, sinks, cos_t, sin1_t, sin2_t, batch, seq):
    t = sw.shape[0]
    Q = SW_BLOCK
    nb = seq // Q
    blk = lambda b, n, s: (b * nb + n, 0)
    return pl.pallas_call(
        _swa_kernel,
        grid_spec=pltpu.PrefetchScalarGridSpec(
            num_scalar_prefetch=1,
            grid=(batch, nb),
            in_specs=[pl.BlockSpec((Q, SW_Q_W), blk),
                      pl.BlockSpec((Q, 2 * SW_KV_W), lambda b, n, s: (b * nb + n, SW_Q_W // (2 * SW_KV_W))),
                      pl.BlockSpec((Q, LANES), blk), pl.BlockSpec((Q, LANES), blk),
                      pl.BlockSpec((Q, LANES), blk)],
            out_specs=pl.BlockSpec((Q, SW_Q_W), blk),
            scratch_shapes=[pltpu.VMEM((SW_KV_HEADS, 2 * Q, LANES), BF16),
                            pltpu.VMEM((2 * SW_KV_HEADS, 2 * Q, LANES), BF16)]),
        out_shape=jax.ShapeDtypeStruct((t, SW_Q_W), BF16),
        compiler_params=pltpu.CompilerParams(dimension_semantics=("arbitrary", "arbitrary"),
                                             vmem_limit_bytes=VMEM_LIMIT),
        name="swa",
    )(sinks, sw, sw, cos_t, sin1_t, sin2_t)


def _merge_kernel(x_ref, odn_ref, osw_ref, gate_ref, wdn_ref, wsw_ref, wo_ref, g_ref, out_ref):
    y_a = jnp.dot(odn_ref[...], wdn_ref[...], preferred_element_type=F32)
    y_b = jnp.dot(osw_ref[...], wsw_ref[...], preferred_element_type=F32)
    ga = gate_ref[:, :D_MODEL].astype(F32)
    gb = gate_ref[:, D_MODEL:].astype(F32)
    mix = _sigmoid(ga) * y_a + _sigmoid(gb) * y_b
    y = jnp.dot(mix.astype(BF16), wo_ref[...], preferred_element_type=F32)
    out_ref[...] = x_ref[...] + _rms(y, g_ref[...])


def _merge(x, o_dn, o_sw, gates, w_dn, w_sw, w_o, g, tm=512):
    t = x.shape[0]
    row = lambda i: (i, 0)
    const = lambda i: (0, 0)
    wspec = pl.BlockSpec((D_MODEL, D_MODEL), const)
    return pl.pallas_call(
        _merge_kernel,
        grid=(t // tm,),
        in_specs=[pl.BlockSpec((tm, D_MODEL), row), pl.BlockSpec((tm, DN_W), row),
                  pl.BlockSpec((tm, SW_Q_W), row), pl.BlockSpec((tm, GATE_COLS), row),
                  wspec, wspec, wspec, pl.BlockSpec((1, D_MODEL), const)],
        out_specs=pl.BlockSpec((tm, D_MODEL), row),
        out_shape=jax.ShapeDtypeStruct((t, D_MODEL), F32),
        compiler_params=pltpu.CompilerParams(dimension_semantics=("arbitrary",),
                                             vmem_limit_bytes=VMEM_LIMIT),
        name="merge",
    )(x, o_dn, o_sw, gates, w_dn, w_sw, w_o, g)


def _mlp_kernel(x_ref, gpre_ref, w1_ref, w2_ref, gpost_ref, out_ref, h_scr, acc_scr):
    f = pl.program_id(1)

    @pl.when(f == 0)
    def _():
        h_scr[...] = _rms(x_ref[...], gpre_ref[...]).astype(BF16)
        acc_scr[...] = jnp.zeros_like(acc_scr)

    a = jnp.maximum(jnp.dot(h_scr[...], w1_ref[...], preferred_element_type=F32), 0.0)
    acc_scr[...] += jnp.dot((a * a).astype(BF16), w2_ref[...], preferred_element_type=F32)

    @pl.when(f == pl.num_programs(1) - 1)
    def _():
        out_ref[...] = x_ref[...] + _rms(acc_scr[...], gpost_ref[...])


def _mlp(x, g_pre, w1, w2, g_post, tm=1024, tf=512):
    t = x.shape[0]
    row = lambda i, f: (i, 0)
    const = lambda i, f: (0, 0)
    return pl.pallas_call(
        _mlp_kernel,
        grid=(t // tm, D_FF // tf),
        in_specs=[pl.BlockSpec((tm, D_MODEL), row), pl.BlockSpec((1, D_MODEL), const),
                  pl.BlockSpec((D_MODEL, tf), lambda i, f: (0, f)),
                  pl.BlockSpec((tf, D_MODEL), lambda i, f: (f, 0)),
                  pl.BlockSpec((1, D_MODEL), const)],
        out_specs=pl.BlockSpec((tm, D_MODEL), row),
        out_shape=jax.ShapeDtypeStruct((t, D_MODEL), F32),
        scratch_shapes=[pltpu.VMEM((tm, D_MODEL), BF16), pltpu.VMEM((tm, D_MODEL), F32)],
        compiler_params=pltpu.CompilerParams(dimension_semantics=("arbitrary", "arbitrary"),
                                             vmem_limit_bytes=VMEM_LIMIT),
        name="mlp",
    )(x, g_pre, w1, w2, g_post)


def _pack_w_in(w_in):
    o = 0
    dn = w_in[:, o:o + DN_COLS]
    o += DN_COLS
    ba = w_in[:, o:o + 2 * DN_HEADS]
    o += 2 * DN_HEADS
    sw = w_in[:, o:o + SW_COLS]
    o += SW_COLS
    gates = w_in[:, o:o + GATE_COLS]
    ba = jnp.pad(ba, ((0, 0), (0, BA_COLS - 2 * DN_HEADS)))
    return jnp.concatenate([dn, sw, gates, ba], axis=1).astype(BF16)


def _lane_row(vec, offset):
    return jnp.zeros((1, LANES), F32).at[0, offset:offset + vec.shape[0]].set(vec.astype(F32))


def kernel(x, positions, pre_mix_g, w_in, dn_conv_w, dn_a_log, dn_dt_bias, dn_norm_g, sw_sinks,
           w_up_dn, w_up_sw, w_o, post_mix_g, pre_mlp_g, w_ff1, w_ff2, post_mlp_g):
    batch, seq, _ = x.shape
    depth = w_in.shape[0]
    t = batch * seq
    xf = x.reshape(t, D_MODEL)
    cos_t, sin1_t, sin2_t = _rope_tables(positions)
    for l in range(depth):
        dn, sw, gates, ba = _inproj(xf, pre_mix_g[l].reshape(1, D_MODEL), _pack_w_in(w_in[l]))
        o_dn = _deltanet(dn, ba, dn_conv_w[l], _lane_row(dn_a_log[l], DN_HEADS),
                         _lane_row(dn_dt_bias[l], DN_HEADS), dn_norm_g[l].reshape(1, DN_DV),
                         batch, seq)
        o_sw = _swa(sw, sw_sinks[l].astype(F32), cos_t, sin1_t, sin2_t, batch, seq)
        xf = _merge(xf, o_dn, o_sw, gates, w_up_dn[l].astype(BF16), w_up_sw[l].astype(BF16),
                    w_o[l].astype(BF16), post_mix_g[l].reshape(1, D_MODEL))
        xf = _mlp(xf, pre_mlp_g[l].reshape(1, D_MODEL), w_ff1[l].astype(BF16),
                  w_ff2[l].astype(BF16), post_mlp_g[l].reshape(1, D_MODEL))
    return xf.reshape(batch, seq, D_MODEL)
```

```python
import functools

import numpy as np
import jax
import jax.numpy as jnp
from jax import lax
from jax.experimental import pallas as pl
from jax.experimental.pallas import tpu as pltpu

F32 = jnp.float32
BF16 = jnp.bfloat16

D_MODEL = 1024
DN_HEADS = 8
DN_DK = 128
DN_DV = 128
DN_CONV = 4
SW_Q_HEADS = 16
SW_KV_HEADS = 2
SW_HEAD_DIM = 64
SW_BLOCK = 128
ROPE_THETA = 500000.0
ROT_DIM = SW_HEAD_DIM // 4
D_FF = 4 * D_MODEL
EPS = 1e-6

DN_W = DN_HEADS * DN_DK
SW_Q_W = SW_Q_HEADS * SW_HEAD_DIM
SW_KV_W = SW_KV_HEADS * SW_HEAD_DIM
LANES = 128

DN_COLS = 4 * DN_W
SW_COLS = SW_Q_W + 2 * SW_KV_W
GATE_COLS = 2 * D_MODEL
BA_COLS = LANES
PACKED_COLS = DN_COLS + SW_COLS + GATE_COLS + BA_COLS

DN_CHUNK = 128
DN_GROUP = 8
VMEM_LIMIT = 56 * 1024 * 1024


def _bdot(a, b):
    return jnp.dot(a.astype(BF16), b.astype(BF16), preferred_element_type=F32)


def _bdot_nt(a, b):
    return lax.dot_general(a.astype(BF16), b.astype(BF16), (((1,), (1,)), ((), ())),
                           preferred_element_type=F32)


def _bdot_tn(a, b):
    return lax.dot_general(a.astype(BF16), b.astype(BF16), (((0,), (0,)), ((), ())),
                           preferred_element_type=F32)


def _rms(x, g):
    return x * lax.rsqrt(jnp.mean(x * x, axis=-1, keepdims=True) + EPS) * g


def _sigmoid(x):
    return 1.0 / (1.0 + jnp.exp(-x))


def _inproj_kernel(x_ref, g_ref, w_ref, dn_ref, sw_ref, gate_ref, ba_ref):
    h = _rms(x_ref[...], g_ref[...]).astype(BF16)
    off = 0
    for ref, width in ((dn_ref, DN_COLS), (sw_ref, SW_COLS), (gate_ref, GATE_COLS),
                       (ba_ref, BA_COLS)):
        for c in range(0, width, 512):
            w = min(512, width - c)
            ref[:, c:c + w] = jnp.dot(h, w_ref[:, off + c:off + c + w],
                                      preferred_element_type=F32).astype(ref.dtype)
        off += width


def _inproj(x, g, w, tm=512):
    t = x.shape[0]
    row = lambda i: (i, 0)
    const = lambda i: (0, 0)
    return pl.pallas_call(
        _inproj_kernel,
        grid=(t // tm,),
        in_specs=[pl.BlockSpec((tm, D_MODEL), row),
                  pl.BlockSpec((1, D_MODEL), const),
                  pl.BlockSpec((D_MODEL, PACKED_COLS), const, pipeline_mode=pl.Buffered(1))],
        out_specs=[pl.BlockSpec((tm, DN_COLS), row), pl.BlockSpec((tm, SW_COLS), row),
                   pl.BlockSpec((tm, GATE_COLS), row), pl.BlockSpec((tm, BA_COLS), row)],
        out_shape=[jax.ShapeDtypeStruct((t, DN_COLS), BF16),
                   jax.ShapeDtypeStruct((t, SW_COLS), BF16),
                   jax.ShapeDtypeStruct((t, GATE_COLS), BF16),
                   jax.ShapeDtypeStruct((t, BA_COLS), F32)],
        compiler_params=pltpu.CompilerParams(dimension_semantics=("arbitrary",),
                                             vmem_limit_bytes=VMEM_LIMIT),
        name="inproj",
    )(x, g, w)


def _deltanet_kernel(dn_ref, ba_ref, cw_ref, alog_ref, dtb_ref, ng_ref, o_ref,
                     state_ref, tail_ref, buf_ref):
    C = DN_CHUNK
    n = pl.program_id(1)

    @pl.when(n == 0)
    def _():
        state_ref[...] = jnp.zeros_like(state_ref)
        tail_ref[...] = jnp.zeros_like(tail_ref)

    def conv_silu(c0):
        cols = slice(c0, c0 + LANES)
        x = dn_ref[:, cols].astype(F32)
        buf_ref[0:8, cols] = tail_ref[:, cols]
        buf_ref[8:8 + C, cols] = x
        y = x * cw_ref[3:4, cols]
        for j in range(DN_CONV - 1):
            y = y + buf_ref[5 + j:5 + j + C, cols] * cw_ref[j:j + 1, cols]
        tail_ref[:, cols] = x[C - 8:C, :]
        return y * _sigmoid(y)

    def l2n(t):
        return t * lax.rsqrt(jnp.sum(t * t, axis=-1, keepdims=True) + EPS)

    ba = ba_ref[...]
    lane = lax.broadcasted_iota(jnp.int32, (C, LANES), 1)
    row = lax.broadcasted_iota(jnp.int32, (C, LANES), 0)
    beta_all = _sigmoid(ba)
    xg = ba + dtb_ref[...]
    softplus = jnp.maximum(xg, 0.0) + jnp.log(1.0 + jnp.exp(-jnp.abs(xg)))
    g_all = -jnp.exp(alog_ref[...]) * softplus

    g_hi = g_all.astype(BF16)
    r1 = g_all - g_hi.astype(F32)
    g_mid = r1.astype(BF16)
    g_lo = (r1 - g_mid.astype(F32)).astype(BF16)
    tril = jnp.where(row >= lane, 1.0, 0.0).astype(BF16)
    gc_all = (jnp.dot(tril, g_hi, preferred_element_type=F32)
              + jnp.dot(tril, g_mid, preferred_element_type=F32)
              + jnp.dot(tril, g_lo, preferred_element_type=F32))
    gc_all_t = gc_all.T

    causal = row >= lane
    strict = row > lane
    eye = jnp.where(row == lane, 1.0, 0.0)
    sib = row ^ lane

    for h0 in range(0, DN_HEADS, DN_GROUP):
        hs = range(h0, h0 + DN_GROUP)
        q = [l2n(conv_silu(h * DN_DK)) * (DN_DK ** -0.5) for h in hs]
        k = [l2n(conv_silu(DN_W + h * DN_DK)) for h in hs]
        v = [conv_silu(2 * DN_W + h * DN_DV) for h in hs]
        beta = [jnp.broadcast_to(beta_all[:, h:h + 1], (C, LANES)) for h in hs]
        gc = [jnp.broadcast_to(gc_all[:, DN_HEADS + h:DN_HEADS + h + 1], (C, LANES)) for h in hs]
        gc_t = [jnp.broadcast_to(gc_all_t[DN_HEADS + h:DN_HEADS + h + 1, :], (C, LANES))
                for h in hs]
        g_last = [g[C - 1:C, :] for g in gc]
        G = range(DN_GROUP)
        decay = [jnp.where(causal, jnp.exp(jnp.where(causal, gc[i] - gc_t[i], 0.0)), 0.0)
                 for i in G]
        eg = [jnp.exp(gc[i]) for i in G]
        kb = [k[i] * beta[i] for i in G]
        a_mat = [jnp.where(strict, _bdot_nt(kb[i], k[i]) * decay[i], 0.0) for i in G]
        t_mat = [eye - jnp.where(sib < 2, a_mat[i], 0.0) for i in G]
        s = 2
        while s < C:
            off = (sib >= s) & (sib < 2 * s)
            x_mat = [_bdot(jnp.where(off, a_mat[i], 0.0), t_mat[i]) for i in G]
            t_mat = [t_mat[i] - _bdot(t_mat[i], x_mat[i]) for i in G]
            s *= 2

        uw = [_bdot(t_mat[i], jnp.concatenate([v[i] * beta[i], kb[i] * eg[i]], axis=1))
              for i in G]
        a_intra = [jnp.where(causal, _bdot_nt(q[i], k[i]) * decay[i], 0.0) for i in G]
        state = [state_ref[h] for h in hs]
        ws = [_bdot(jnp.concatenate([uw[i][:, DN_DV:], q[i] * eg[i]], axis=0), state[i])
              for i in G]
        v_new = [uw[i][:, :DN_DV] - ws[i][:C] for i in G]
        o = [ws[i][C:] + _bdot(a_intra[i], v_new[i]) for i in G]
        kv = [_bdot_tn(k[i] * jnp.exp(g_last[i] - gc[i]), v_new[i]) for i in G]
        for i, h in enumerate(hs):
            state_ref[h] = state[i] * jnp.exp(g_last[i]) + kv[i]
            zc = slice(3 * DN_W + h * DN_DV, 3 * DN_W + (h + 1) * DN_DV)
            z = dn_ref[:, zc].astype(F32)
            o_ref[:, h * DN_DV:(h + 1) * DN_DV] = (
                _rms(o[i], ng_ref[...]) * (z * _sigmoid(z))).astype(o_ref.dtype)


def _deltanet(dn, ba, conv_w, alog_row, dtb_row, norm_g, batch, seq):
    t = dn.shape[0]
    C = DN_CHUNK
    nc = seq // C
    blk = lambda b, n: (b * nc + n, 0)
    const = lambda b, n: (0, 0)
    return pl.pallas_call(
        _deltanet_kernel,
        grid=(batch, nc),
        in_specs=[pl.BlockSpec((C, DN_COLS), blk),
                  pl.BlockSpec((C, LANES), blk),
                  pl.BlockSpec((DN_CONV, 3 * DN_W), const),
                  pl.BlockSpec((1, LANES), const), pl.BlockSpec((1, LANES), const),
                  pl.BlockSpec((1, LANES), const)],
        out_specs=pl.BlockSpec((C, DN_W), blk),
        out_shape=jax.ShapeDtypeStruct((t, DN_W), BF16),
        scratch_shapes=[pltpu.VMEM((DN_HEADS, DN_DK, DN_DV), F32),
                        pltpu.VMEM((8, 3 * DN_W), F32),
                        pltpu.VMEM((8 + C, 3 * DN_W), F32)],
        compiler_params=pltpu.CompilerParams(
            dimension_semantics=("arbitrary", "arbitrary"),
            vmem_limit_bytes=VMEM_LIMIT),
        name="deltanet",
    )(dn, ba, conv_w, alog_row, dtb_row, norm_g)


def _rope_table_kernel(pos_ref, freq_ref, mcos_ref, msin1_ref, msin2_ref,
                       c_ref, s1_ref, s2_ref):
    ang = pos_ref[...].astype(F32) * freq_ref[...]
    cos = jnp.cos(ang)
    sin = jnp.sin(ang)
    c_ref[...] = jnp.where(mcos_ref[...] > 0.5, cos, 1.0)
    s1_ref[...] = jnp.where(msin1_ref[...] > 0.5, -sin, 0.0)
    s2_ref[...] = jnp.where(msin2_ref[...] > 0.5, sin, 0.0)


def _rope_tables(positions, tm=1024):
    t = positions.size
    half = ROT_DIM // 2
    lane = np.arange(LANES)
    within = lane % SW_HEAD_DIM
    inv_freq = ROPE_THETA ** (-np.arange(half, dtype=np.float32) * (2.0 / ROT_DIM))
    freq = np.where(within < ROT_DIM, inv_freq[within % half], 0.0).astype(np.float32)
    mcos = (within < ROT_DIM).astype(np.float32)
    msin1 = (within < half).astype(np.float32)
    msin2 = ((within >= half) & (within < ROT_DIM)).astype(np.float32)
    rows = [jnp.asarray(a.reshape(1, LANES)) for a in (freq, mcos, msin1, msin2)]
    const = lambda i: (0, 0)
    row = lambda i: (i, 0)
    out = jax.ShapeDtypeStruct((t, LANES), F32)
    return pl.pallas_call(
        _rope_table_kernel,
        grid=(t // tm,),
        in_specs=[pl.BlockSpec((tm, 1), row)] + [pl.BlockSpec((1, LANES), const)] * 4,
        out_specs=[pl.BlockSpec((tm, LANES), row)] * 3,
        out_shape=[out, out, out],
        name="rope_tables",
    )(positions.reshape(t, 1), *rows)


def _swa_kernel(sink_ref, q_ref, kv_ref, c_ref, s1_ref, s2_ref, o_ref, k_scr, v_scr):
    Q = SW_BLOCK
    n = pl.program_id(1)
    half = SW_HEAD_DIM
    cos = c_ref[...]
    sin1 = s1_ref[...]
    sin2 = s2_ref[...]
    lane = lax.broadcasted_iota(jnp.int32, (Q, LANES), 1)
    low = lane < half

    def rope(x):
        return x * cos + pltpu.roll(x, LANES - ROT_DIM // 2, 1) * sin1 \
            + pltpu.roll(x, ROT_DIM // 2, 1) * sin2

    @pl.when(n == 0)
    def _():
        k_scr[...] = jnp.zeros_like(k_scr)
        v_scr[...] = jnp.zeros_like(v_scr)

    @pl.when(n > 0)
    def _():
        for j in range(SW_KV_HEADS):
            k_scr[j, 0:Q, :] = k_scr[j, Q:2 * Q, :]
            v_scr[2 * j, 0:Q, :] = v_scr[2 * j, Q:2 * Q, :]
            v_scr[2 * j + 1, 0:Q, :] = v_scr[2 * j + 1, Q:2 * Q, :]

    kv = kv_ref[...].astype(F32)
    kr = rope(kv[:, :LANES])
    kr_sw = pltpu.roll(kr, half, 1)
    vt = kv[:, LANES:]
    vt_sw = pltpu.roll(vt, half, 1)
    k_scr[0, Q:2 * Q, :] = jnp.where(low, kr, kr_sw).astype(BF16)
    k_scr[1, Q:2 * Q, :] = jnp.where(low, kr_sw, kr).astype(BF16)
    v_scr[0, Q:2 * Q, :] = jnp.where(low, vt, 0.0).astype(BF16)
    v_scr[1, Q:2 * Q, :] = jnp.where(low, 0.0, vt_sw).astype(BF16)
    v_scr[2, Q:2 * Q, :] = jnp.where(low, vt_sw, 0.0).astype(BF16)
    v_scr[3, Q:2 * Q, :] = jnp.where(low, 0.0, vt).astype(BF16)

    qi = lax.broadcasted_iota(jnp.int32, (Q, 2 * Q), 0)
    ki = lax.broadcasted_iota(jnp.int32, (Q, 2 * Q), 1)
    valid = (ki > qi) & (ki <= qi + Q) & ((ki >= Q) | (n > 0))

    def attend(qh, keys, vals, sink):
        s = jnp.where(valid, _bdot_nt(qh, keys), -jnp.inf)
        m = jnp.maximum(jnp.max(s, axis=-1, keepdims=True), sink)
        p = jnp.exp(s - m)
        denom = jnp.sum(p, axis=-1, keepdims=True) + jnp.exp(sink - m)
        return _bdot(p, vals) * (1.0 / denom)

    pairs_per_kv = SW_Q_HEADS // SW_KV_HEADS // 2
    for pair in range(SW_Q_HEADS // 2):
        j = pair // pairs_per_kv
        qp = rope(q_ref[:, pair * LANES:(pair + 1) * LANES].astype(F32)) * (SW_HEAD_DIM ** -0.5)
        keys = k_scr[j]
        o_lo = attend(jnp.where(low, qp, 0.0), keys, v_scr[2 * j], sink_ref[2 * pair])
        o_hi = attend(jnp.where(low, 0.0, qp), keys, v_scr[2 * j + 1], sink_ref[2 * pair + 1])
        o_ref[:, pair * LANES:(pair + 1) * LANES] = (o_lo + o_hi).astype(o_ref.dtype)


def _swa(sw, sinks, cos_t, sin1_t, sin2_t, batch, seq):
    t = sw.shape[0]
    Q = SW_BLOCK
    nb = seq // Q
    blk = lambda b, n, s: (b * nb + n, 0)
    return pl.pallas_call(
        _swa_kernel,
        grid_spec=pltpu.PrefetchScalarGridSpec(
            num_scalar_prefetch=1,
            grid=(batch, nb),
            in_specs=[pl.BlockSpec((Q, SW_Q_W), blk),
                      pl.BlockSpec((Q, 2 * SW_KV_W), lambda b, n, s: (b * nb + n, SW_Q_W // (2 * SW_KV_W))),
                      pl.BlockSpec((Q, LANES), blk), pl.BlockSpec((Q, LANES), blk),
                      pl.BlockSpec((Q, LANES), blk)],
            out_specs=pl.BlockSpec((Q, SW_Q_W), blk),
            scratch_shapes=[pltpu.VMEM((SW_KV_HEADS, 2 * Q, LANES), BF16),
                            pltpu.VMEM((2 * SW_KV_HEADS, 2 * Q, LANES), BF16)]),
        out_shape=jax.ShapeDtypeStruct((t, SW_Q_W), BF16),
        compiler_params=pltpu.CompilerParams(dimension_semantics=("arbitrary", "arbitrary"),
                                             vmem_limit_bytes=VMEM_LIMIT),
        name="swa",
    )(sinks, sw, sw, cos_t, sin1_t, sin2_t)


def _merge_kernel(x_ref, odn_ref, osw_ref, gate_ref, wdn_ref, wsw_ref, wo_ref, g_ref, out_ref):
    y_a = jnp.dot(odn_ref[...], wdn_ref[...], preferred_element_type=F32)
    y_b = jnp.dot(osw_ref[...], wsw_ref[...], preferred_element_type=F32)
    ga = gate_ref[:, :D_MODEL].astype(F32)
    gb = gate_ref[:, D_MODEL:].astype(F32)
    mix = _sigmoid(ga) * y_a + _sigmoid(gb) * y_b
    y = jnp.dot(mix.astype(BF16), wo_ref[...], preferred_element_type=F32)
    out_ref[...] = x_ref[...] + _rms(y, g_ref[...])


def _merge(x, o_dn, o_sw, gates, w_dn, w_sw, w_o, g, tm=512):
    t = x.shape[0]
    row = lambda i: (i, 0)
    const = lambda i: (0, 0)
    wspec = pl.BlockSpec((D_MODEL, D_MODEL), const)
    return pl.pallas_call(
        _merge_kernel,
        grid=(t // tm,),
        in_specs=[pl.BlockSpec((tm, D_MODEL), row), pl.BlockSpec((tm, DN_W), row),
                  pl.BlockSpec((tm, SW_Q_W), row), pl.BlockSpec((tm, GATE_COLS), row),
                  wspec, wspec, wspec, pl.BlockSpec((1, D_MODEL), const)],
        out_specs=pl.BlockSpec((tm, D_MODEL), row),
        out_shape=jax.ShapeDtypeStruct((t, D_MODEL), F32),
        compiler_params=pltpu.CompilerParams(dimension_semantics=("arbitrary",),
                                             vmem_limit_bytes=VMEM_LIMIT),
        name="merge",
    )(x, o_dn, o_sw, gates, w_dn, w_sw, w_o, g)


def _mlp_kernel(x_ref, gpre_ref, w1_ref, w2_ref, gpost_ref, out_ref, h_scr, acc_scr):
    f = pl.program_id(1)

    @pl.when(f == 0)
    def _():
        h_scr[...] = _rms(x_ref[...], gpre_ref[...]).astype(BF16)
        acc_scr[...] = jnp.zeros_like(acc_scr)

    a = jnp.maximum(jnp.dot(h_scr[...], w1_ref[...], preferred_element_type=F32), 0.0)
    acc_scr[...] += jnp.dot((a * a).astype(BF16), w2_ref[...], preferred_element_type=F32)

    @pl.when(f == pl.num_programs(1) - 1)
    def _():
        out_ref[...] = x_ref[...] + _rms(acc_scr[...], gpost_ref[...])


def _mlp(x, g_pre, w1, w2, g_post, tm=1024, tf=512):
    t = x.shape[0]
    row = lambda i, f: (i, 0)
    const = lambda i, f: (0, 0)
    return pl.pallas_call(
        _mlp_kernel,
        grid=(t // tm, D_FF // tf),
        in_specs=[pl.BlockSpec((tm, D_MODEL), row), pl.BlockSpec((1, D_MODEL), const),
                  pl.BlockSpec((D_MODEL, tf), lambda i, f: (0, f)),
                  pl.BlockSpec((tf, D_MODEL), lambda i, f: (f, 0)),
                  pl.BlockSpec((1, D_MODEL), const)],
        out_specs=pl.BlockSpec((tm, D_MODEL), row),
        out_shape=jax.ShapeDtypeStruct((t, D_MODEL), F32),
        scratch_shapes=[pltpu.VMEM((tm, D_MODEL), BF16), pltpu.VMEM((tm, D_MODEL), F32)],
        compiler_params=pltpu.CompilerParams(dimension_semantics=("arbitrary", "arbitrary"),
                                             vmem_limit_bytes=VMEM_LIMIT),
        name="mlp",
    )(x, g_pre, w1, w2, g_post)


def _pack_w_in(w_in):
    o = 0
    dn = w_in[:, o:o + DN_COLS]
    o += DN_COLS
    ba = w_in[:, o:o + 2 * DN_HEADS]
    o += 2 * DN_HEADS
    sw = w_in[:, o:o + SW_COLS]
    o += SW_COLS
    gates = w_in[:, o:o + GATE_COLS]
    ba = jnp.pad(ba, ((0, 0), (0, BA_COLS - 2 * DN_HEADS)))
    return jnp.concatenate([dn, sw, gates, ba], axis=1).astype(BF16)


def _lane_row(vec, offset):
    return jnp.zeros((1, LANES), F32).at[0, offset:offset + vec.shape[0]].set(vec.astype(F32))


def kernel(x, positions, pre_mix_g, w_in, dn_conv_w, dn_a_log, dn_dt_bias, dn_norm_g, sw_sinks,
           w_up_dn, w_up_sw, w_o, post_mix_g, pre_mlp_g, w_ff1, w_ff2, post_mlp_g):
    batch, seq, _ = x.shape
    depth = w_in.shape[0]
    t = batch * seq
    xf = x.reshape(t, D_MODEL)
    cos_t, sin1_t, sin2_t = _rope_tables(positions)
    for l in range(depth):
        dn, sw, gates, ba = _inproj(xf, pre_mix_g[l].reshape(1, D_MODEL), _pack_w_in(w_in[l]))
        o_dn = _deltanet(dn, ba, dn_conv_w[l], _lane_row(dn_a_log[l], DN_HEADS),
                         _lane_row(dn_dt_bias[l], DN_HEADS), dn_norm_g[l].reshape(1, DN_DV),
                         batch, seq)
        o_sw = _swa(sw, sw_sinks[l].astype(F32), cos_t, sin1_t, sin2_t, batch, seq)
        xf = _merge(xf, o_dn, o_sw, gates, w_up_dn[l].astype(BF16), w_up_sw[l].astype(BF16),
                    w_o[l].astype(BF16), post_mix_g[l].reshape(1, D_MODEL))
        xf = _mlp(xf, pre_mlp_g[l].reshape(1, D_MODEL), w_ff1[l].astype(BF16),
                  w_ff2[l].astype(BF16), post_mlp_g[l].reshape(1, D_MODEL))
    return xf.reshape(batch, seq, D_MODEL)
```

```python
import functools

import numpy as np
import jax
import jax.numpy as jnp
from jax import lax
from jax.experimental import pallas as pl
from jax.experimental.pallas import tpu as pltpu

F32 = jnp.float32
BF16 = jnp.bfloat16

D_MODEL = 1024
DN_HEADS = 8
DN_DK = 128
DN_DV = 128
DN_CONV = 4
SW_Q_HEADS = 16
SW_KV_HEADS = 2
SW_HEAD_DIM = 64
SW_BLOCK = 128
ROPE_THETA = 500000.0
ROT_DIM = SW_HEAD_DIM // 4
D_FF = 4 * D_MODEL
EPS = 1e-6

DN_W = DN_HEADS * DN_DK
SW_Q_W = SW_Q_HEADS * SW_HEAD_DIM
SW_KV_W = SW_KV_HEADS * SW_HEAD_DIM
LANES = 128

DN_COLS = 4 * DN_W
SW_COLS = SW_Q_W + 2 * SW_KV_W
GATE_COLS = 2 * D_MODEL
BA_COLS = LANES
PACKED_COLS = DN_COLS + SW_COLS + GATE_COLS + BA_COLS

DN_CHUNK = 128
INPROJ_CHUNK = 256
VMEM_LIMIT = 56 * 1024 * 1024


def _bdot(a, b):
    return jnp.dot(a.astype(BF16), b.astype(BF16), preferred_element_type=F32)


def _bdot_nt(a, b):
    return lax.dot_general(a.astype(BF16), b.astype(BF16), (((1,), (1,)), ((), ())),
                           preferred_element_type=F32)


def _bdot_tn(a, b):
    return lax.dot_general(a.astype(BF16), b.astype(BF16), (((0,), (0,)), ((), ())),
                           preferred_element_type=F32)


def _rms(x, g):
    return x * lax.rsqrt(jnp.mean(x * x, axis=-1, keepdims=True) + EPS) * g


def _sigmoid(x):
    return 1.0 / (1.0 + jnp.exp(-x))


def _inproj_kernel(x_ref, g_ref, w_ref, cw_ref, qkv_ref, z_ref, sw_ref, gate_ref, ba_ref,
                   tail_ref, *, tiles_per_seq):
    tm = x_ref.shape[0]

    @pl.when(pl.program_id(0) % tiles_per_seq == 0)
    def _():
        tail_ref[...] = jnp.zeros_like(tail_ref)

    h = _rms(x_ref[...], g_ref[...]).astype(BF16)

    def proj(off, width):
        return jnp.dot(h, w_ref[:, off:off + width], preferred_element_type=F32)

    def conv_chunk(c):
        cols = slice(c, c + INPROJ_CHUNK)
        p = proj(c, INPROJ_CHUNK)
        ext = jnp.concatenate([tail_ref[:, cols], p], axis=0)
        tail_ref[:, cols] = p[tm - 8:, :]
        y = p * cw_ref[DN_CONV - 1:DN_CONV, cols]
        for j in range(DN_CONV - 1):
            sh = DN_CONV - 1 - j
            y = y + ext[8 - sh:8 - sh + tm, :] * cw_ref[j:j + 1, cols]
        y = y * _sigmoid(y)
        for hh in range(0, INPROJ_CHUNK, DN_DK):
            t = y[:, hh:hh + DN_DK]
            if c + hh < 2 * DN_W:
                scale = DN_DK ** -0.5 if c + hh < DN_W else 1.0
                t = t * (lax.rsqrt(jnp.sum(t * t, axis=-1, keepdims=True) + EPS) * scale)
            qkv_ref[:, c + hh:c + hh + DN_DK] = t.astype(qkv_ref.dtype)

    def silu_chunk(c):
        zc = proj(3 * DN_W + c, INPROJ_CHUNK)
        z_ref[:, c:c + INPROJ_CHUNK] = (zc * _sigmoid(zc)).astype(z_ref.dtype)

    def plain_chunk(ref, off, c, w):
        ref[:, c:c + w] = proj(off + c, w).astype(ref.dtype)

    heavy = [functools.partial(conv_chunk, c) for c in range(0, 3 * DN_W, INPROJ_CHUNK)]
    light = [functools.partial(silu_chunk, c) for c in range(0, DN_W, INPROJ_CHUNK)]
    off = DN_COLS
    for ref, width in ((sw_ref, SW_COLS), (gate_ref, GATE_COLS), (ba_ref, BA_COLS)):
        for c in range(0, width, INPROJ_CHUNK):
            light.append(functools.partial(plain_chunk, ref, off, c, min(INPROJ_CHUNK, width - c)))
        off += width
    per_heavy = -(-len(light) // len(heavy))
    for i, fn in enumerate(heavy):
        fn()
        for g in light[i * per_heavy:(i + 1) * per_heavy]:
            g()


def _inproj(x, g, w, layer, conv_w, seq, tm=512):
    t = x.shape[0]
    row = lambda i: (i, 0)
    const = lambda i: (0, 0)
    return pl.pallas_call(
        functools.partial(_inproj_kernel, tiles_per_seq=seq // tm),
        grid=(t // tm,),
        in_specs=[pl.BlockSpec((tm, D_MODEL), row),
                  pl.BlockSpec((1, D_MODEL), const),
                  pl.BlockSpec((None, D_MODEL, PACKED_COLS), lambda i: (layer, 0, 0),
                               pipeline_mode=pl.Buffered(1)),
                  pl.BlockSpec((DN_CONV, 3 * DN_W), const)],
        out_specs=[pl.BlockSpec((tm, 3 * DN_W), row), pl.BlockSpec((tm, DN_W), row),
                   pl.BlockSpec((tm, SW_COLS), row),
                   pl.BlockSpec((tm, GATE_COLS), row), pl.BlockSpec((tm, BA_COLS), row)],
        out_shape=[jax.ShapeDtypeStruct((t, 3 * DN_W), BF16),
                   jax.ShapeDtypeStruct((t, DN_W), BF16),
                   jax.ShapeDtypeStruct((t, SW_COLS), BF16),
                   jax.ShapeDtypeStruct((t, GATE_COLS), BF16),
                   jax.ShapeDtypeStruct((t, BA_COLS), F32)],
        scratch_shapes=[pltpu.VMEM((8, 3 * DN_W), F32)],
        compiler_params=pltpu.CompilerParams(dimension_semantics=("arbitrary",),
                                             vmem_limit_bytes=VMEM_LIMIT),
        name="inproj",
    )(x, g, w, conv_w)


def _deltanet_kernel(qkv_ref, z_ref, ba_ref, alog_ref, dtb_ref, ng_ref, o_ref, state_ref):
    C = DN_CHUNK

    @pl.when(pl.program_id(1) == 0)
    def _():
        state_ref[...] = jnp.zeros_like(state_ref)

    ba = ba_ref[...]
    lane = lax.broadcasted_iota(jnp.int32, (C, LANES), 1)
    row = lax.broadcasted_iota(jnp.int32, (C, LANES), 0)
    causal = row >= lane
    strict = row > lane
    eye = jnp.where(row == lane, 1.0, 0.0)
    sib = row ^ lane
    beta_all = _sigmoid(ba).astype(BF16)
    xg = ba + dtb_ref[...]
    softplus = jnp.maximum(xg, 0.0) + jnp.log(1.0 + jnp.exp(-jnp.abs(xg)))
    g_all = -jnp.exp(alog_ref[...]) * softplus
    g_hi = g_all.astype(BF16)
    r1 = g_all - g_hi.astype(F32)
    g_mid = r1.astype(BF16)
    g_lo = (r1 - g_mid.astype(F32)).astype(BF16)
    tril = jnp.where(causal, 1.0, 0.0).astype(BF16)
    gc_all = (jnp.dot(tril, g_hi, preferred_element_type=F32)
              + jnp.dot(tril, g_mid, preferred_element_type=F32)
              + jnp.dot(tril, g_lo, preferred_element_type=F32))
    gc_all_t = gc_all.T
    g_last_all = gc_all[C - 1:C, :]
    eg_all = jnp.exp(gc_all).astype(BF16)
    ed_all = jnp.exp(g_last_all - gc_all).astype(BF16)
    el_all = jnp.exp(g_last_all)

    H = range(DN_HEADS)
    q = [qkv_ref[:, h * DN_DK:(h + 1) * DN_DK] for h in H]
    k = [qkv_ref[:, DN_W + h * DN_DK:DN_W + (h + 1) * DN_DK] for h in H]
    v = [qkv_ref[:, 2 * DN_W + h * DN_DV:2 * DN_W + (h + 1) * DN_DV] for h in H]

    def lane_bcast(x, j):
        return jnp.broadcast_to(x[:, j:j + 1], (x.shape[0], LANES))

    beta = [lane_bcast(beta_all, h) for h in H]
    eg = [lane_bcast(eg_all, DN_HEADS + h) for h in H]
    ed = [lane_bcast(ed_all, DN_HEADS + h) for h in H]
    el = [lane_bcast(el_all, DN_HEADS + h) for h in H]
    gc = [lane_bcast(gc_all, DN_HEADS + h) for h in H]
    gc_t = [jnp.broadcast_to(gc_all_t[DN_HEADS + h:DN_HEADS + h + 1, :], (C, LANES)) for h in H]
    decay = [jnp.where(causal, jnp.exp(jnp.where(causal, gc[h] - gc_t[h], 0.0)), 0.0) for h in H]
    kb = [k[h] * beta[h] for h in H]
    kk = [_bdot_nt(kb[h], k[h]) for h in H]
    qk = [_bdot_nt(q[h], k[h]) for h in H]
    a_mat = [jnp.where(strict, kk[h] * decay[h], 0.0) for h in H]
    t_mat = [eye - jnp.where(sib < 2, a_mat[h], 0.0) for h in H]
    size = 2
    while size < C:
        off = (sib >= size) & (sib < 2 * size)
        x_mat = [_bdot(jnp.where(off, a_mat[h], 0.0), t_mat[h]) for h in H]
        t_mat = [t_mat[h] - _bdot(t_mat[h], x_mat[h]) for h in H]
        size *= 2

    uw = [_bdot(t_mat[h], jnp.concatenate([v[h] * beta[h], kb[h] * eg[h]], axis=1))
          for h in H]
    a_intra = [jnp.where(causal, qk[h] * decay[h], 0.0) for h in H]
    state = [state_ref[h] for h in H]
    ws = [_bdot(jnp.concatenate([uw[h][:, DN_DV:].astype(BF16), q[h] * eg[h]], axis=0), state[h])
          for h in H]
    v_new = [uw[h][:, :DN_DV] - ws[h][:C] for h in H]
    o = [ws[h][C:] + _bdot(a_intra[h], v_new[h]) for h in H]
    kv = [_bdot_tn(k[h] * ed[h], v_new[h]) for h in H]
    for h in H:
        state_ref[h] = state[h] * el[h] + kv[h]
        o_ref[:, h * DN_DV:(h + 1) * DN_DV] = (
            _rms(o[h], ng_ref[...]) * z_ref[:, h * DN_DV:(h + 1) * DN_DV].astype(F32)
        ).astype(o_ref.dtype)


def _deltanet(qkv, z, ba, alog_row, dtb_row, norm_g, batch, seq):
    t = qkv.shape[0]
    C = DN_CHUNK
    nc = seq // C
    blk = lambda b, n: (b * nc + n, 0)
    const = lambda b, n: (0, 0)
    return pl.pallas_call(
        _deltanet_kernel,
        grid=(batch, nc),
        in_specs=[pl.BlockSpec((C, 3 * DN_W), blk),
                  pl.BlockSpec((C, DN_W), blk),
                  pl.BlockSpec((C, LANES), blk),
                  pl.BlockSpec((1, LANES), const), pl.BlockSpec((1, LANES), const),
                  pl.BlockSpec((1, LANES), const)],
        out_specs=pl.BlockSpec((C, DN_W), blk),
        out_shape=jax.ShapeDtypeStruct((t, DN_W), BF16),
        scratch_shapes=[pltpu.VMEM((DN_HEADS, DN_DK, DN_DV), F32)],
        compiler_params=pltpu.CompilerParams(dimension_semantics=("arbitrary", "arbitrary"),
                                             vmem_limit_bytes=VMEM_LIMIT),
        name="deltanet",
    )(qkv, z, ba, alog_row, dtb_row, norm_g)


def _rope_table_kernel(pos_ref, freq_ref, mcos_ref, msin1_ref, msin2_ref,
                       c_ref, s1_ref, s2_ref):
    ang = pos_ref[...].astype(F32) * freq_ref[...]
    cos = jnp.cos(ang)
    sin = jnp.sin(ang)
    c_ref[...] = jnp.where(mcos_ref[...] > 0.5, cos, 1.0)
    s1_ref[...] = jnp.where(msin1_ref[...] > 0.5, -sin, 0.0)
    s2_ref[...] = jnp.where(msin2_ref[...] > 0.5, sin, 0.0)


def _rope_tables(positions, tm=1024):
    t = positions.size
    half = ROT_DIM // 2
    lane = np.arange(LANES)
    within = lane % SW_HEAD_DIM
    inv_freq = ROPE_THETA ** (-np.arange(half, dtype=np.float32) * (2.0 / ROT_DIM))
    freq = np.where(within < ROT_DIM, inv_freq[within % half], 0.0).astype(np.float32)
    mcos = (within < ROT_DIM).astype(np.float32)
    msin1 = (within < half).astype(np.float32)
    msin2 = ((within >= half) & (within < ROT_DIM)).astype(np.float32)
    rows = [jnp.asarray(a.reshape(1, LANES)) for a in (freq, mcos, msin1, msin2)]
    const = lambda i: (0, 0)
    row = lambda i: (i, 0)
    out = jax.ShapeDtypeStruct((t, LANES), F32)
    return pl.pallas_call(
        _rope_table_kernel,
        grid=(t // tm,),
        in_specs=[pl.BlockSpec((tm, 1), row)] + [pl.BlockSpec((1, LANES), const)] * 4,
        out_specs=[pl.BlockSpec((tm, LANES), row)] * 3,
        out_shape=[out, out, out],
        name="rope_tables",
    )(positions.reshape(t, 1), *rows)


def _swa_kernel(sink_ref, q_ref, kv_ref, c_ref, s1_ref, s2_ref, o_ref, k_scr, v_scr):
    Q = SW_BLOCK
    n = pl.program_id(1)
    half = SW_HEAD_DIM
    cos = c_ref[...]
    sin1 = s1_ref[...]
    sin2 = s2_ref[...]
    lane = lax.broadcasted_iota(jnp.int32, (Q, LANES), 1)
    low = lane < half

    def rope(x):
        return x * cos + pltpu.roll(x, LANES - ROT_DIM // 2, 1) * sin1 \
            + pltpu.roll(x, ROT_DIM // 2, 1) * sin2

    @pl.when(n == 0)
    def _():
        k_scr[...] = jnp.zeros_like(k_scr)
        v_scr[...] = jnp.zeros_like(v_scr)

    @pl.when(n > 0)
    def _():
        for j in range(SW_KV_HEADS):
            k_scr[j, 0:Q, :] = k_scr[j, Q:2 * Q, :]
            v_scr[2 * j, 0:Q, :] = v_scr[2 * j, Q:2 * Q, :]
            v_scr[2 * j + 1, 0:Q, :] = v_scr[2 * j + 1, Q:2 * Q, :]

    kv = kv_ref[...].astype(F32)
    kr = rope(kv[:, :LANES])
    kr_sw = pltpu.roll(kr, half, 1)
    vt = kv[:, LANES:]
    vt_sw = pltpu.roll(vt, half, 1)
    k_scr[0, Q:2 * Q, :] = jnp.where(low, kr, kr_sw).astype(BF16)
    k_scr[1, Q:2 * Q, :] = jnp.where(low, kr_sw, kr).astype(BF16)
    v_scr[0, Q:2 * Q, :] = jnp.where(low, vt, 0.0).astype(BF16)
    v_scr[1, Q:2 * Q, :] = jnp.where(low, 0.0, vt_sw).astype(BF16)
    v_scr[2, Q:2 * Q, :] = jnp.where(low, vt_sw, 0.0).astype(BF16)
    v_scr[3, Q:2 * Q, :] = jnp.where(low, 0.0, vt).astype(BF16)

    qi = lax.broadcasted_iota(jnp.int32, (Q, 2 * Q), 0)
    ki = lax.broadcasted_iota(jnp.int32, (Q, 2 * Q), 1)
    valid = (ki > qi) & (ki <= qi + Q) & ((ki >= Q) | (n > 0))

    def attend(qh, keys, vals, sink):
        s = jnp.where(valid, _bdot_nt(qh, keys), -jnp.inf)
        m = jnp.maximum(jnp.max(s, axis=-1, keepdims=True), sink)
        p = jnp.exp(s - m)
        denom = jnp.sum(p, axis=-1, keepdims=True) + jnp.exp(sink - m)
        return _bdot(p, vals) * (1.0 / denom)

    pairs_per_kv = SW_Q_HEADS // SW_KV_HEADS // 2
    for pair in range(SW_Q_HEADS // 2):
        j = pair // pairs_per_kv
        qp = rope(q_ref[:, pair * LANES:(pair + 1) * LANES].astype(F32)) * (SW_HEAD_DIM ** -0.5)
        keys = k_scr[j]
        o_lo = attend(jnp.where(low, qp, 0.0), keys, v_scr[2 * j], sink_ref[2 * pair])
        o_hi = attend(jnp.where(low, 0.0, qp), keys, v_scr[2 * j + 1], sink_ref[2 * pair + 1])
        o_ref[:, pair * LANES:(pair + 1) * LANES] = (o_lo + o_hi).astype(o_ref.dtype)


def _swa(sw, sinks, cos_t, sin1_t, sin2_t, batch, seq):
    t = sw.shape[0]
    Q = SW_BLOCK
    nb = seq // Q
    blk = lambda b, n, s: (b * nb + n, 0)
    return pl.pallas_call(
        _swa_kernel,
        grid_spec=pltpu.PrefetchScalarGridSpec(
            num_scalar_prefetch=1,
            grid=(batch, nb),
            in_specs=[pl.BlockSpec((Q, SW_Q_W), blk),
                      pl.BlockSpec((Q, 2 * SW_KV_W), lambda b, n, s: (b * nb + n, SW_Q_W // (2 * SW_KV_W))),
                      pl.BlockSpec((Q, LANES), blk), pl.BlockSpec((Q, LANES), blk),
                      pl.BlockSpec((Q, LANES), blk)],
            out_specs=pl.BlockSpec((Q, SW_Q_W), blk),
            scratch_shapes=[pltpu.VMEM((SW_KV_HEADS, 2 * Q, LANES), BF16),
                            pltpu.VMEM((2 * SW_KV_HEADS, 2 * Q, LANES), BF16)]),
        out_shape=jax.ShapeDtypeStruct((t, SW_Q_W), BF16),
        compiler_params=pltpu.CompilerParams(dimension_semantics=("arbitrary", "arbitrary"),
                                             vmem_limit_bytes=VMEM_LIMIT),
        name="swa",
    )(sinks, sw, sw, cos_t, sin1_t, sin2_t)


def _merge_kernel(x_ref, odn_ref, osw_ref, gate_ref, wdn_ref, wsw_ref, wo_ref, g_ref, out_ref):
    y_a = jnp.dot(odn_ref[...], wdn_ref[...], preferred_element_type=F32)
    y_b = jnp.dot(osw_ref[...], wsw_ref[...], preferred_element_type=F32)
    ga = gate_ref[:, :D_MODEL].astype(F32)
    gb = gate_ref[:, D_MODEL:].astype(F32)
    mix = _sigmoid(ga) * y_a + _sigmoid(gb) * y_b
    y = jnp.dot(mix.astype(BF16), wo_ref[...], preferred_element_type=F32)
    out_ref[...] = x_ref[...] + _rms(y, g_ref[...])


def _merge(x, o_dn, o_sw, gates, w_dn, w_sw, w_o, layer, g, tm=512):
    t = x.shape[0]
    row = lambda i: (i, 0)
    const = lambda i: (0, 0)
    wspec = pl.BlockSpec((None, D_MODEL, D_MODEL), lambda i: (layer, 0, 0))
    return pl.pallas_call(
        _merge_kernel,
        grid=(t // tm,),
        in_specs=[pl.BlockSpec((tm, D_MODEL), row), pl.BlockSpec((tm, DN_W), row),
                  pl.BlockSpec((tm, SW_Q_W), row), pl.BlockSpec((tm, GATE_COLS), row),
                  wspec, wspec, wspec, pl.BlockSpec((1, D_MODEL), const)],
        out_specs=pl.BlockSpec((tm, D_MODEL), row),
        out_shape=jax.ShapeDtypeStruct((t, D_MODEL), F32),
        compiler_params=pltpu.CompilerParams(dimension_semantics=("arbitrary",),
                                             vmem_limit_bytes=VMEM_LIMIT),
        name="merge",
    )(x, o_dn, o_sw, gates, w_dn, w_sw, w_o, g)


def _mlp_kernel(x_ref, gpre_ref, w1_ref, w2_ref, gpost_ref, out_ref, h_scr, acc_scr):
    f = pl.program_id(1)

    @pl.when(f == 0)
    def _():
        h_scr[...] = _rms(x_ref[...], gpre_ref[...]).astype(BF16)
        acc_scr[...] = jnp.zeros_like(acc_scr)

    a = jnp.maximum(jnp.dot(h_scr[...], w1_ref[...], preferred_element_type=F32), 0.0)
    acc_scr[...] += jnp.dot((a * a).astype(BF16), w2_ref[...], preferred_element_type=F32)

    @pl.when(f == pl.num_programs(1) - 1)
    def _():
        out_ref[...] = x_ref[...] + _rms(acc_scr[...], gpost_ref[...])


def _mlp(x, g_pre, w1, w2, layer, g_post, tm=1024, tf=512):
    t = x.shape[0]
    row = lambda i, f: (i, 0)
    const = lambda i, f: (0, 0)
    return pl.pallas_call(
        _mlp_kernel,
        grid=(t // tm, D_FF // tf),
        in_specs=[pl.BlockSpec((tm, D_MODEL), row), pl.BlockSpec((1, D_MODEL), const),
                  pl.BlockSpec((None, D_MODEL, tf), lambda i, f: (layer, 0, f)),
                  pl.BlockSpec((None, tf, D_MODEL), lambda i, f: (layer, f, 0)),
                  pl.BlockSpec((1, D_MODEL), const)],
        out_specs=pl.BlockSpec((tm, D_MODEL), row),
        out_shape=jax.ShapeDtypeStruct((t, D_MODEL), F32),
        scratch_shapes=[pltpu.VMEM((tm, D_MODEL), BF16), pltpu.VMEM((tm, D_MODEL), F32)],
        compiler_params=pltpu.CompilerParams(dimension_semantics=("arbitrary", "arbitrary"),
                                             vmem_limit_bytes=VMEM_LIMIT),
        name="mlp",
    )(x, g_pre, w1, w2, g_post)


def _pack_w_in(w_in):
    o = 0
    dn = w_in[..., o:o + DN_COLS]
    o += DN_COLS
    ba = w_in[..., o:o + 2 * DN_HEADS]
    o += 2 * DN_HEADS
    sw = w_in[..., o:o + SW_COLS]
    o += SW_COLS
    gates = w_in[..., o:o + GATE_COLS]
    ba = jnp.pad(ba, [(0, 0)] * (ba.ndim - 1) + [(0, BA_COLS - 2 * DN_HEADS)])
    return jnp.concatenate([dn, sw, gates, ba], axis=-1).astype(BF16)


def _lane_row(vec, offset):
    return jnp.zeros((1, LANES), F32).at[0, offset:offset + vec.shape[0]].set(vec.astype(F32))


def kernel(x, positions, pre_mix_g, w_in, dn_conv_w, dn_a_log, dn_dt_bias, dn_norm_g, sw_sinks,
           w_up_dn, w_up_sw, w_o, post_mix_g, pre_mlp_g, w_ff1, w_ff2, post_mlp_g):
    batch, seq, _ = x.shape
    depth = w_in.shape[0]
    t = batch * seq
    xf = x.reshape(t, D_MODEL)
    cos_t, sin1_t, sin2_t = _rope_tables(positions)
    w_in_p = _pack_w_in(w_in)
    w_dn, w_sw, w_out = w_up_dn.astype(BF16), w_up_sw.astype(BF16), w_o.astype(BF16)
    w1, w2 = w_ff1.astype(BF16), w_ff2.astype(BF16)
    for l in range(depth):
        qkv, z, sw, gates, ba = _inproj(xf, pre_mix_g[l].reshape(1, D_MODEL), w_in_p, l,
                                        dn_conv_w[l], seq)
        o_dn = _deltanet(qkv, z, ba, _lane_row(dn_a_log[l], DN_HEADS),
                         _lane_row(dn_dt_bias[l], DN_HEADS), dn_norm_g[l].reshape(1, DN_DV),
                         batch, seq)
        o_sw = _swa(sw, sw_sinks[l].astype(F32), cos_t, sin1_t, sin2_t, batch, seq)
        xf = _merge(xf, o_dn, o_sw, gates, w_dn, w_sw, w_out, l,
                    post_mix_g[l].reshape(1, D_MODEL))
        xf = _mlp(xf, pre_mlp_g[l].reshape(1, D_MODEL), w1, w2, l,
                  post_mlp_g[l].reshape(1, D_MODEL))
    return xf.reshape(batch, seq, D_MODEL)
```

```python
import functools

import numpy as np
import jax
import jax.numpy as jnp
from jax import lax
from jax.experimental import pallas as pl
from jax.experimental.pallas import tpu as pltpu

F32 = jnp.float32
BF16 = jnp.bfloat16

D_MODEL = 1024
DN_HEADS = 8
DN_DK = 128
DN_DV = 128
DN_CONV = 4
SW_Q_HEADS = 16
SW_KV_HEADS = 2
SW_HEAD_DIM = 64
SW_BLOCK = 128
SW_PAIR_GROUP = 8
ROPE_THETA = 500000.0
ROT_DIM = SW_HEAD_DIM // 4
D_FF = 4 * D_MODEL
EPS = 1e-6

DN_W = DN_HEADS * DN_DK
SW_Q_W = SW_Q_HEADS * SW_HEAD_DIM
SW_KV_W = SW_KV_HEADS * SW_HEAD_DIM
LANES = 128

DN_COLS = 4 * DN_W
SW_COLS = SW_Q_W + 2 * SW_KV_W
GATE_COLS = 2 * D_MODEL
BA_COLS = LANES
PACKED_COLS = DN_COLS + SW_COLS + GATE_COLS + BA_COLS

DN_CHUNK = 128
INPROJ_CHUNK = 256
VMEM_LIMIT = 56 * 1024 * 1024


def _bdot(a, b):
    return jnp.dot(a.astype(BF16), b.astype(BF16), preferred_element_type=F32)


def _bdot_nt(a, b):
    return lax.dot_general(a.astype(BF16), b.astype(BF16), (((1,), (1,)), ((), ())),
                           preferred_element_type=F32)


def _bdot_tn(a, b):
    return lax.dot_general(a.astype(BF16), b.astype(BF16), (((0,), (0,)), ((), ())),
                           preferred_element_type=F32)


def _rms(x, g):
    return x * lax.rsqrt(jnp.mean(x * x, axis=-1, keepdims=True) + EPS) * g


def _sigmoid(x):
    return 1.0 / (1.0 + jnp.exp(-x))


def _inproj_kernel(x_ref, g_ref, w_ref, cw_ref, qkv_ref, z_ref, sw_ref, gate_ref, ba_ref,
                   tail_ref, pbuf_ref, ybuf_ref, *, tiles_per_seq):
    tm = x_ref.shape[0]

    @pl.when(pl.program_id(0) % tiles_per_seq == 0)
    def _():
        tail_ref[...] = jnp.zeros_like(tail_ref)

    h = _rms(x_ref[...], g_ref[...]).astype(BF16)

    def proj(off, width):
        return jnp.dot(h, w_ref[:, off:off + width], preferred_element_type=F32)

    def conv_matmul(c):
        p = proj(c, INPROJ_CHUNK)
        for hh in range(0, INPROJ_CHUNK, DN_DK):
            pbuf_ref[(c + hh) // DN_DK] = p[:, hh:hh + DN_DK]

    def conv_epilogue(c):
        rows = tm // 8
        for hh in range(0, INPROJ_CHUNK, DN_DK):
            g = (c + hh) // DN_DK
            cols = slice(c + hh, c + hh + DN_DK)
            x = [pbuf_ref[g, pl.ds(j, rows, stride=8), :] for j in range(8)]
            prev = {}
            for j in range(8 - (DN_CONV - 1), 8):
                prev[j - 8] = jnp.concatenate([tail_ref[j:j + 1, cols], x[j][:rows - 1, :]],
                                              axis=0)
            tail_ref[:, cols] = pbuf_ref[g, tm - 8:tm, :]
            tok = lambda j: x[j] if j >= 0 else prev[j]
            for j in range(8):
                y = x[j] * cw_ref[DN_CONV - 1:DN_CONV, cols]
                for i in range(DN_CONV - 1):
                    y = y + tok(j - (DN_CONV - 1 - i)) * cw_ref[i:i + 1, cols]
                y = y * _sigmoid(y)
                if c + hh < 2 * DN_W:
                    scale = DN_DK ** -0.5 if c + hh < DN_W else 1.0
                    y = y * (lax.rsqrt(jnp.sum(y * y, axis=-1, keepdims=True) + EPS) * scale)
                ybuf_ref[g, pl.ds(j, rows, stride=8), :] = y
            qkv_ref[:, cols] = ybuf_ref[g].astype(qkv_ref.dtype)

    def silu_chunk(c):
        zc = proj(3 * DN_W + c, INPROJ_CHUNK)
        z_ref[:, c:c + INPROJ_CHUNK] = (zc * _sigmoid(zc)).astype(z_ref.dtype)

    def plain_chunk(ref, off, c, w):
        ref[:, c:c + w] = proj(off + c, w).astype(ref.dtype)

    conv_cols = list(range(0, 3 * DN_W, INPROJ_CHUNK))
    light = [functools.partial(silu_chunk, c) for c in range(0, DN_W, INPROJ_CHUNK)]
    off = DN_COLS
    for ref, width in ((sw_ref, SW_COLS), (gate_ref, GATE_COLS), (ba_ref, BA_COLS)):
        for c in range(0, width, INPROJ_CHUNK):
            light.append(functools.partial(plain_chunk, ref, off, c, min(INPROJ_CHUNK, width - c)))
        off += width
    per_conv = -(-len(light) // len(conv_cols))
    conv_matmul(conv_cols[0])
    for i, c in enumerate(conv_cols):
        for fn in light[i * per_conv:(i + 1) * per_conv]:
            fn()
        if i + 1 < len(conv_cols):
            conv_matmul(conv_cols[i + 1])
        conv_epilogue(c)


def _inproj(x, g, w, layer, conv_w, seq, tm=512):
    t = x.shape[0]
    row = lambda i: (i, 0)
    const = lambda i: (0, 0)
    return pl.pallas_call(
        functools.partial(_inproj_kernel, tiles_per_seq=seq // tm),
        grid=(t // tm,),
        in_specs=[pl.BlockSpec((tm, D_MODEL), row),
                  pl.BlockSpec((1, D_MODEL), const),
                  pl.BlockSpec((None, D_MODEL, PACKED_COLS), lambda i: (layer, 0, 0),
                               pipeline_mode=pl.Buffered(1)),
                  pl.BlockSpec((DN_CONV, 3 * DN_W), const)],
        out_specs=[pl.BlockSpec((tm, 3 * DN_W), row), pl.BlockSpec((tm, DN_W), row),
                   pl.BlockSpec((tm, SW_COLS), row),
                   pl.BlockSpec((tm, GATE_COLS), row), pl.BlockSpec((tm, BA_COLS), row)],
        out_shape=[jax.ShapeDtypeStruct((t, 3 * DN_W), BF16),
                   jax.ShapeDtypeStruct((t, DN_W), BF16),
                   jax.ShapeDtypeStruct((t, SW_COLS), BF16),
                   jax.ShapeDtypeStruct((t, GATE_COLS), BF16),
                   jax.ShapeDtypeStruct((t, BA_COLS), F32)],
        scratch_shapes=[pltpu.VMEM((8, 3 * DN_W), F32), pltpu.VMEM((3 * DN_HEADS, tm, LANES), F32),
                        pltpu.VMEM((3 * DN_HEADS, tm, LANES), F32)],
        compiler_params=pltpu.CompilerParams(dimension_semantics=("arbitrary",),
                                             vmem_limit_bytes=VMEM_LIMIT),
        name="inproj",
    )(x, g, w, conv_w)


def _deltanet_kernel(qkv_ref, z_ref, ba_ref, alog_ref, dtb_ref, ng_ref, o_ref, state_ref):
    C = DN_CHUNK

    @pl.when(pl.program_id(1) == 0)
    def _():
        state_ref[...] = jnp.zeros_like(state_ref)

    ba = ba_ref[...]
    lane = lax.broadcasted_iota(jnp.int32, (C, LANES), 1)
    row = lax.broadcasted_iota(jnp.int32, (C, LANES), 0)
    causal = row >= lane
    strict = row > lane
    eye = jnp.where(row == lane, 1.0, 0.0)
    sib = row ^ lane
    beta_all = _sigmoid(ba).astype(BF16)
    xg = ba + dtb_ref[...]
    softplus = jnp.maximum(xg, 0.0) + jnp.log(1.0 + jnp.exp(-jnp.abs(xg)))
    g_all = -jnp.exp(alog_ref[...]) * softplus
    g_hi = g_all.astype(BF16)
    r1 = g_all - g_hi.astype(F32)
    g_mid = r1.astype(BF16)
    g_lo = (r1 - g_mid.astype(F32)).astype(BF16)
    tril = jnp.where(causal, 1.0, 0.0).astype(BF16)
    gc_all = (jnp.dot(tril, g_hi, preferred_element_type=F32)
              + jnp.dot(tril, g_mid, preferred_element_type=F32)
              + jnp.dot(tril, g_lo, preferred_element_type=F32))
    gc_all_t = gc_all.T
    g_last_all = gc_all[C - 1:C, :]
    eg_all = jnp.exp(gc_all).astype(BF16)
    ed_all = jnp.exp(g_last_all - gc_all).astype(BF16)
    el_all = jnp.exp(g_last_all)

    H = range(DN_HEADS)
    q = [qkv_ref[:, h * DN_DK:(h + 1) * DN_DK] for h in H]
    k = [qkv_ref[:, DN_W + h * DN_DK:DN_W + (h + 1) * DN_DK] for h in H]
    v = [qkv_ref[:, 2 * DN_W + h * DN_DV:2 * DN_W + (h + 1) * DN_DV] for h in H]

    def lane_bcast(x, j):
        return jnp.broadcast_to(x[:, j:j + 1], (x.shape[0], LANES))

    beta = [lane_bcast(beta_all, h) for h in H]
    eg = [lane_bcast(eg_all, DN_HEADS + h) for h in H]
    ed = [lane_bcast(ed_all, DN_HEADS + h) for h in H]
    el = [lane_bcast(el_all, DN_HEADS + h) for h in H]
    gc = [lane_bcast(gc_all, DN_HEADS + h) for h in H]
    gc_t = [jnp.broadcast_to(gc_all_t[DN_HEADS + h:DN_HEADS + h + 1, :], (C, LANES)) for h in H]
    decay = [jnp.where(causal, jnp.exp(jnp.where(causal, gc[h] - gc_t[h], 0.0)), 0.0) for h in H]
    kb = [k[h] * beta[h] for h in H]
    kk = [_bdot_nt(kb[h], k[h]) for h in H]
    qk = [_bdot_nt(q[h], k[h]) for h in H]
    a_mat = [jnp.where(strict, kk[h] * decay[h], 0.0) for h in H]
    t_mat = [eye - jnp.where(sib < 2, a_mat[h], 0.0) for h in H]
    size = 2
    while size < C:
        off = (sib >= size) & (sib < 2 * size)
        x_mat = [_bdot(jnp.where(off, a_mat[h], 0.0), t_mat[h]) for h in H]
        t_mat = [t_mat[h] - _bdot(t_mat[h], x_mat[h]) for h in H]
        size *= 2

    uw = [_bdot(t_mat[h], jnp.concatenate([v[h] * beta[h], kb[h] * eg[h]], axis=1))
          for h in H]
    a_intra = [jnp.where(causal, qk[h] * decay[h], 0.0) for h in H]
    state = [state_ref[h] for h in H]
    ws = [_bdot(jnp.concatenate([uw[h][:, DN_DV:].astype(BF16), q[h] * eg[h]], axis=0), state[h])
          for h in H]
    v_new = [uw[h][:, :DN_DV] - ws[h][:C] for h in H]
    o = [ws[h][C:] + _bdot(a_intra[h], v_new[h]) for h in H]
    kv = [_bdot_tn(k[h] * ed[h], v_new[h]) for h in H]
    for h in H:
        state_ref[h] = state[h] * el[h] + kv[h]
        o_ref[:, h * DN_DV:(h + 1) * DN_DV] = (
            _rms(o[h], ng_ref[...]) * z_ref[:, h * DN_DV:(h + 1) * DN_DV].astype(F32)
        ).astype(o_ref.dtype)


def _deltanet(qkv, z, ba, alog_row, dtb_row, norm_g, batch, seq):
    t = qkv.shape[0]
    C = DN_CHUNK
    nc = seq // C
    blk = lambda b, n: (b * nc + n, 0)
    const = lambda b, n: (0, 0)
    return pl.pallas_call(
        _deltanet_kernel,
        grid=(batch, nc),
        in_specs=[pl.BlockSpec((C, 3 * DN_W), blk),
                  pl.BlockSpec((C, DN_W), blk),
                  pl.BlockSpec((C, LANES), blk),
                  pl.BlockSpec((1, LANES), const), pl.BlockSpec((1, LANES), const),
                  pl.BlockSpec((1, LANES), const)],
        out_specs=pl.BlockSpec((C, DN_W), blk),
        out_shape=jax.ShapeDtypeStruct((t, DN_W), BF16),
        scratch_shapes=[pltpu.VMEM((DN_HEADS, DN_DK, DN_DV), F32)],
        compiler_params=pltpu.CompilerParams(dimension_semantics=("arbitrary", "arbitrary"),
                                             vmem_limit_bytes=VMEM_LIMIT),
        name="deltanet",
    )(qkv, z, ba, alog_row, dtb_row, norm_g)


def _rope_table_kernel(pos_ref, freq_ref, mcos_ref, msin1_ref, msin2_ref,
                       c_ref, s1_ref, s2_ref):
    ang = pos_ref[...].astype(F32) * freq_ref[...]
    cos = jnp.cos(ang)
    sin = jnp.sin(ang)
    c_ref[...] = jnp.where(mcos_ref[...] > 0.5, cos, 1.0)
    s1_ref[...] = jnp.where(msin1_ref[...] > 0.5, -sin, 0.0)
    s2_ref[...] = jnp.where(msin2_ref[...] > 0.5, sin, 0.0)


def _rope_tables(positions, tm=1024):
    t = positions.size
    half = ROT_DIM // 2
    lane = np.arange(LANES)
    within = lane % SW_HEAD_DIM
    inv_freq = ROPE_THETA ** (-np.arange(half, dtype=np.float32) * (2.0 / ROT_DIM))
    freq = np.where(within < ROT_DIM, inv_freq[within % half], 0.0).astype(np.float32)
    mcos = (within < ROT_DIM).astype(np.float32)
    msin1 = (within < half).astype(np.float32)
    msin2 = ((within >= half) & (within < ROT_DIM)).astype(np.float32)
    rows = [jnp.asarray(a.reshape(1, LANES)) for a in (freq, mcos, msin1, msin2)]
    const = lambda i: (0, 0)
    row = lambda i: (i, 0)
    out = jax.ShapeDtypeStruct((t, LANES), F32)
    return pl.pallas_call(
        _rope_table_kernel,
        grid=(t // tm,),
        in_specs=[pl.BlockSpec((tm, 1), row)] + [pl.BlockSpec((1, LANES), const)] * 4,
        out_specs=[pl.BlockSpec((tm, LANES), row)] * 3,
        out_shape=[out, out, out],
        name="rope_tables",
    )(positions.reshape(t, 1), *rows)


def _swa_kernel(sink_ref, q_ref, kv_ref, c_ref, s1_ref, s2_ref, o_ref, k_scr, v_scr):
    Q = SW_BLOCK
    n = pl.program_id(1)
    half = SW_HEAD_DIM
    cos = c_ref[...]
    sin1 = s1_ref[...]
    sin2 = s2_ref[...]
    lane = lax.broadcasted_iota(jnp.int32, (Q, LANES), 1)
    low = lane < half
    log2e = 1.4426950408889634

    def rope(x):
        return x * cos + pltpu.roll(x, LANES - ROT_DIM // 2, 1) * sin1 \
            + pltpu.roll(x, ROT_DIM // 2, 1) * sin2

    @pl.when(n == 0)
    def _():
        k_scr[...] = jnp.zeros_like(k_scr)
        v_scr[...] = jnp.zeros_like(v_scr)

    @pl.when(n > 0)
    def _():
        for j in range(SW_KV_HEADS):
            k_scr[j, 0:Q, :] = k_scr[j, Q:2 * Q, :]
            v_scr[2 * j, 0:Q, :] = v_scr[2 * j, Q:2 * Q, :]
            v_scr[2 * j + 1, 0:Q, :] = v_scr[2 * j + 1, Q:2 * Q, :]

    kv = kv_ref[...].astype(F32)
    kr = rope(kv[:, :LANES])
    kr_sw = pltpu.roll(kr, half, 1)
    vt = kv[:, LANES:]
    vt_sw = pltpu.roll(vt, half, 1)
    ones_lo = jnp.where(lane == half, 1.0, 0.0)
    ones_hi = jnp.where(lane == 0, 1.0, 0.0)
    k_scr[0, Q:2 * Q, :] = jnp.where(low, kr, kr_sw).astype(BF16)
    k_scr[1, Q:2 * Q, :] = jnp.where(low, kr_sw, kr).astype(BF16)
    v_scr[0, Q:2 * Q, :] = jnp.where(low, vt, ones_lo).astype(BF16)
    v_scr[1, Q:2 * Q, :] = jnp.where(low, ones_hi, vt_sw).astype(BF16)
    v_scr[2, Q:2 * Q, :] = jnp.where(low, vt_sw, ones_lo).astype(BF16)
    v_scr[3, Q:2 * Q, :] = jnp.where(low, ones_hi, vt).astype(BF16)

    qi = lax.broadcasted_iota(jnp.int32, (Q, 2 * Q), 0)
    ki = lax.broadcasted_iota(jnp.int32, (Q, 2 * Q), 1)
    valid = (ki > qi) & (ki <= qi + Q) & ((ki >= Q) | (n > 0))

    pairs_per_kv = SW_Q_HEADS // SW_KV_HEADS // 2
    for p0 in range(0, SW_Q_HEADS // 2, SW_PAIR_GROUP):
        heads = []
        for pair in range(p0, p0 + SW_PAIR_GROUP):
            heads += [(pair, pair // pairs_per_kv, False), (pair, pair // pairs_per_kv, True)]
        qp = {pair: rope(q_ref[:, pair * LANES:(pair + 1) * LANES].astype(F32))
              * (SW_HEAD_DIM ** -0.5 * log2e) for pair in range(p0, p0 + SW_PAIR_GROUP)}
        s = [_bdot_nt(jnp.where(low, 0.0, qp[pair]) if hi else jnp.where(low, qp[pair], 0.0),
                      k_scr[j]) for pair, j, hi in heads]
        sink = [sink_ref[2 * pair + int(hi)] * log2e for pair, j, hi in heads]
        s = [jnp.where(valid, t, -jnp.inf) for t in s]
        m = [jnp.maximum(jnp.max(s[i], axis=-1, keepdims=True), sink[i])
             for i in range(len(heads))]
        p = [jnp.exp2(s[i] - m[i]) for i in range(len(heads))]
        pv = [_bdot(p[i], v_scr[2 * j + int(hi)]) for i, (pair, j, hi) in enumerate(heads)]
        out = []
        for i, (pair, j, hi) in enumerate(heads):
            sum_lane = 0 if hi else half
            denom = pv[i][:, sum_lane:sum_lane + 1] + jnp.exp2(sink[i] - m[i])
            out.append(pv[i] * (1.0 / denom))
        for i in range(0, len(heads), 2):
            pair = heads[i][0]
            o_ref[:, pair * LANES:(pair + 1) * LANES] = jnp.where(
                low, out[i], out[i + 1]).astype(o_ref.dtype)


def _swa(sw, sinks, cos_t, sin1_t, sin2_t, batch, seq):
    t = sw.shape[0]
    Q = SW_BLOCK
    nb = seq // Q
    blk = lambda b, n, s: (b * nb + n, 0)
    return pl.pallas_call(
        _swa_kernel,
        grid_spec=pltpu.PrefetchScalarGridSpec(
            num_scalar_prefetch=1,
            grid=(batch, nb),
            in_specs=[pl.BlockSpec((Q, SW_Q_W), blk),
                      pl.BlockSpec((Q, 2 * SW_KV_W), lambda b, n, s: (b * nb + n, SW_Q_W // (2 * SW_KV_W))),
                      pl.BlockSpec((Q, LANES), blk), pl.BlockSpec((Q, LANES), blk),
                      pl.BlockSpec((Q, LANES), blk)],
            out_specs=pl.BlockSpec((Q, SW_Q_W), blk),
            scratch_shapes=[pltpu.VMEM((SW_KV_HEADS, 2 * Q, LANES), BF16),
                            pltpu.VMEM((2 * SW_KV_HEADS, 2 * Q, LANES), BF16)]),
        out_shape=jax.ShapeDtypeStruct((t, SW_Q_W), BF16),
        compiler_params=pltpu.CompilerParams(dimension_semantics=("arbitrary", "arbitrary"),
                                             vmem_limit_bytes=VMEM_LIMIT),
        name="swa",
    )(sinks, sw, sw, cos_t, sin1_t, sin2_t)


def _merge_kernel(x_ref, odn_ref, osw_ref, gate_ref, wdn_ref, wsw_ref, wo_ref, g_ref, out_ref):
    y_a = jnp.dot(odn_ref[...], wdn_ref[...], preferred_element_type=F32)
    y_b = jnp.dot(osw_ref[...], wsw_ref[...], preferred_element_type=F32)
    ga = gate_ref[:, :D_MODEL].astype(F32)
    gb = gate_ref[:, D_MODEL:].astype(F32)
    mix = _sigmoid(ga) * y_a + _sigmoid(gb) * y_b
    y = jnp.dot(mix.astype(BF16), wo_ref[...], preferred_element_type=F32)
    out_ref[...] = x_ref[...] + _rms(y, g_ref[...])


def _merge(x, o_dn, o_sw, gates, w_dn, w_sw, w_o, layer, g, tm=512):
    t = x.shape[0]
    row = lambda i: (i, 0)
    const = lambda i: (0, 0)
    wspec = pl.BlockSpec((None, D_MODEL, D_MODEL), lambda i: (layer, 0, 0))
    return pl.pallas_call(
        _merge_kernel,
        grid=(t // tm,),
        in_specs=[pl.BlockSpec((tm, D_MODEL), row), pl.BlockSpec((tm, DN_W), row),
                  pl.BlockSpec((tm, SW_Q_W), row), pl.BlockSpec((tm, GATE_COLS), row),
                  wspec, wspec, wspec, pl.BlockSpec((1, D_MODEL), const)],
        out_specs=pl.BlockSpec((tm, D_MODEL), row),
        out_shape=jax.ShapeDtypeStruct((t, D_MODEL), F32),
        compiler_params=pltpu.CompilerParams(dimension_semantics=("arbitrary",),
                                             vmem_limit_bytes=VMEM_LIMIT),
        name="merge",
    )(x, o_dn, o_sw, gates, w_dn, w_sw, w_o, g)


def _mlp_kernel(x_ref, gpre_ref, w1_ref, w2_ref, gpost_ref, out_ref, h_scr, acc_scr):
    f = pl.program_id(1)

    @pl.when(f == 0)
    def _():
        h_scr[...] = _rms(x_ref[...], gpre_ref[...]).astype(BF16)
        acc_scr[...] = jnp.zeros_like(acc_scr)

    a = jnp.maximum(jnp.dot(h_scr[...], w1_ref[...], preferred_element_type=F32), 0.0)
    acc_scr[...] += jnp.dot((a * a).astype(BF16), w2_ref[...], preferred_element_type=F32)

    @pl.when(f == pl.num_programs(1) - 1)
    def _():
        out_ref[...] = x_ref[...] + _rms(acc_scr[...], gpost_ref[...])


def _mlp(x, g_pre, w1, w2, layer, g_post, tm=1024, tf=1024):
    t = x.shape[0]
    row = lambda i, f: (i, 0)
    const = lambda i, f: (0, 0)
    return pl.pallas_call(
        _mlp_kernel,
        grid=(t // tm, D_FF // tf),
        in_specs=[pl.BlockSpec((tm, D_MODEL), row), pl.BlockSpec((1, D_MODEL), const),
                  pl.BlockSpec((None, D_MODEL, tf), lambda i, f: (layer, 0, f)),
                  pl.BlockSpec((None, tf, D_MODEL), lambda i, f: (layer, f, 0)),
                  pl.BlockSpec((1, D_MODEL), const)],
        out_specs=pl.BlockSpec((tm, D_MODEL), row),
        out_shape=jax.ShapeDtypeStruct((t, D_MODEL), F32),
        scratch_shapes=[pltpu.VMEM((tm, D_MODEL), BF16), pltpu.VMEM((tm, D_MODEL), F32)],
        compiler_params=pltpu.CompilerParams(dimension_semantics=("arbitrary", "arbitrary"),
                                             vmem_limit_bytes=VMEM_LIMIT),
        name="mlp",
    )(x, g_pre, w1, w2, g_post)


def _pack_w_in(w_in):
    o = 0
    dn = w_in[..., o:o + DN_COLS]
    o += DN_COLS
    ba = w_in[..., o:o + 2 * DN_HEADS]
    o += 2 * DN_HEADS
    sw = w_in[..., o:o + SW_COLS]
    o += SW_COLS
    gates = w_in[..., o:o + GATE_COLS]
    ba = jnp.pad(ba, [(0, 0)] * (ba.ndim - 1) + [(0, BA_COLS - 2 * DN_HEADS)])
    return jnp.concatenate([dn, sw, gates, ba], axis=-1).astype(BF16)


def _lane_row(vec, offset):
    return jnp.zeros((1, LANES), F32).at[0, offset:offset + vec.shape[0]].set(vec.astype(F32))


def kernel(x, positions, pre_mix_g, w_in, dn_conv_w, dn_a_log, dn_dt_bias, dn_norm_g, sw_sinks,
           w_up_dn, w_up_sw, w_o, post_mix_g, pre_mlp_g, w_ff1, w_ff2, post_mlp_g):
    batch, seq, _ = x.shape
    depth = w_in.shape[0]
    t = batch * seq
    xf = x.reshape(t, D_MODEL)
    cos_t, sin1_t, sin2_t = _rope_tables(positions)
    w_in_p = _pack_w_in(w_in)
    w_dn, w_sw, w_out = w_up_dn.astype(BF16), w_up_sw.astype(BF16), w_o.astype(BF16)
    w1, w2 = w_ff1.astype(BF16), w_ff2.astype(BF16)
    for l in range(depth):
        qkv, z, sw, gates, ba = _inproj(xf, pre_mix_g[l].reshape(1, D_MODEL), w_in_p, l,
                                        dn_conv_w[l], seq)
        o_dn = _deltanet(qkv, z, ba, _lane_row(dn_a_log[l], DN_HEADS),
                         _lane_row(dn_dt_bias[l], DN_HEADS), dn_norm_g[l].reshape(1, DN_DV),
                         batch, seq)
        o_sw = _swa(sw, sw_sinks[l].astype(F32), cos_t, sin1_t, sin2_t, batch, seq)
        xf = _merge(xf, o_dn, o_sw, gates, w_dn, w_sw, w_out, l,
                    post_mix_g[l].reshape(1, D_MODEL))
        xf = _mlp(xf, pre_mlp_g[l].reshape(1, D_MODEL), w1, w2, l,
                  post_mlp_g[l].reshape(1, D_MODEL))
    return xf.reshape(batch, seq, D_MODEL)
```

```python
import functools

import numpy as np
import jax
import jax.numpy as jnp
from jax import lax
from jax.experimental import pallas as pl
from jax.experimental.pallas import tpu as pltpu

F32 = jnp.float32
BF16 = jnp.bfloat16

D_MODEL = 1024
DN_HEADS = 8
DN_DK = 128
DN_DV = 128
DN_CONV = 4
SW_Q_HEADS = 16
SW_KV_HEADS = 2
SW_HEAD_DIM = 64
SW_BLOCK = 128
SW_PAIR_GROUP = 8
ROPE_THETA = 500000.0
ROT_DIM = SW_HEAD_DIM // 4
D_FF = 4 * D_MODEL
EPS = 1e-6

DN_W = DN_HEADS * DN_DK
SW_Q_W = SW_Q_HEADS * SW_HEAD_DIM
SW_KV_W = SW_KV_HEADS * SW_HEAD_DIM
LANES = 128

DN_COLS = 4 * DN_W
SW_COLS = SW_Q_W + 2 * SW_KV_W
GATE_COLS = 2 * D_MODEL
BA_COLS = LANES
PACKED_COLS = DN_COLS + SW_COLS + GATE_COLS + BA_COLS

DN_CHUNK = 128
INPROJ_CHUNK = 256
VMEM_LIMIT = 56 * 1024 * 1024


def _bdot(a, b):
    return jnp.dot(a.astype(BF16), b.astype(BF16), preferred_element_type=F32)


def _bdot_nt(a, b):
    return lax.dot_general(a.astype(BF16), b.astype(BF16), (((1,), (1,)), ((), ())),
                           preferred_element_type=F32)


def _bdot_tn(a, b):
    return lax.dot_general(a.astype(BF16), b.astype(BF16), (((0,), (0,)), ((), ())),
                           preferred_element_type=F32)


def _rms(x, g):
    return x * lax.rsqrt(jnp.mean(x * x, axis=-1, keepdims=True) + EPS) * g


def _sigmoid(x):
    return 1.0 / (1.0 + jnp.exp(-x))


def _inproj_kernel(x_ref, g_ref, w_ref, cw_ref, qkv_ref, z_ref, sw_ref, gate_ref, ba_ref,
                   tail_ref, pbuf_ref, ybuf_ref, *, tiles_per_seq):
    tm = x_ref.shape[0]

    @pl.when(pl.program_id(0) % tiles_per_seq == 0)
    def _():
        tail_ref[...] = jnp.zeros_like(tail_ref)

    h = _rms(x_ref[...], g_ref[...]).astype(BF16)

    def proj(off, width):
        return jnp.dot(h, w_ref[:, off:off + width], preferred_element_type=F32)

    def conv_matmul(c):
        p = proj(c, INPROJ_CHUNK)
        for hh in range(0, INPROJ_CHUNK, DN_DK):
            pbuf_ref[(c + hh) // DN_DK] = p[:, hh:hh + DN_DK]

    def conv_epilogue(c):
        rows = tm // 8
        for hh in range(0, INPROJ_CHUNK, DN_DK):
            g = (c + hh) // DN_DK
            cols = slice(c + hh, c + hh + DN_DK)
            x = [pbuf_ref[g, pl.ds(j, rows, stride=8), :] for j in range(8)]
            prev = {}
            for j in range(8 - (DN_CONV - 1), 8):
                prev[j - 8] = jnp.concatenate([tail_ref[j:j + 1, cols], x[j][:rows - 1, :]],
                                              axis=0)
            tail_ref[:, cols] = pbuf_ref[g, tm - 8:tm, :]
            tok = lambda j: x[j] if j >= 0 else prev[j]
            for j in range(8):
                y = x[j] * cw_ref[DN_CONV - 1:DN_CONV, cols]
                for i in range(DN_CONV - 1):
                    y = y + tok(j - (DN_CONV - 1 - i)) * cw_ref[i:i + 1, cols]
                y = y * _sigmoid(y)
                if c + hh < 2 * DN_W:
                    scale = DN_DK ** -0.5 if c + hh < DN_W else 1.0
                    y = y * (lax.rsqrt(jnp.sum(y * y, axis=-1, keepdims=True) + EPS) * scale)
                ybuf_ref[g, pl.ds(j, rows, stride=8), :] = y
            qkv_ref[:, cols] = ybuf_ref[g].astype(qkv_ref.dtype)

    def silu_chunk(c):
        zc = proj(3 * DN_W + c, INPROJ_CHUNK)
        z_ref[:, c:c + INPROJ_CHUNK] = (zc * _sigmoid(zc)).astype(z_ref.dtype)

    def plain_chunk(ref, off, c, w):
        ref[:, c:c + w] = proj(off + c, w).astype(ref.dtype)

    conv_cols = list(range(0, 3 * DN_W, INPROJ_CHUNK))
    light = [functools.partial(silu_chunk, c) for c in range(0, DN_W, INPROJ_CHUNK)]
    off = DN_COLS
    for ref, width in ((sw_ref, SW_COLS), (gate_ref, GATE_COLS), (ba_ref, BA_COLS)):
        for c in range(0, width, INPROJ_CHUNK):
            light.append(functools.partial(plain_chunk, ref, off, c, min(INPROJ_CHUNK, width - c)))
        off += width
    per_conv = -(-len(light) // len(conv_cols))
    conv_matmul(conv_cols[0])
    for i, c in enumerate(conv_cols):
        for fn in light[i * per_conv:(i + 1) * per_conv]:
            fn()
        if i + 1 < len(conv_cols):
            conv_matmul(conv_cols[i + 1])
        conv_epilogue(c)


def _inproj(x, g, w, layer, conv_w, seq, tm=512):
    t = x.shape[0]
    row = lambda i: (i, 0)
    const = lambda i: (0, 0)
    return pl.pallas_call(
        functools.partial(_inproj_kernel, tiles_per_seq=seq // tm),
        grid=(t // tm,),
        in_specs=[pl.BlockSpec((tm, D_MODEL), row),
                  pl.BlockSpec((1, D_MODEL), const),
                  pl.BlockSpec((None, D_MODEL, PACKED_COLS), lambda i: (layer, 0, 0),
                               pipeline_mode=pl.Buffered(1)),
                  pl.BlockSpec((DN_CONV, 3 * DN_W), const)],
        out_specs=[pl.BlockSpec((tm, 3 * DN_W), row), pl.BlockSpec((tm, DN_W), row),
                   pl.BlockSpec((tm, SW_COLS), row),
                   pl.BlockSpec((tm, GATE_COLS), row), pl.BlockSpec((tm, BA_COLS), row)],
        out_shape=[jax.ShapeDtypeStruct((t, 3 * DN_W), BF16),
                   jax.ShapeDtypeStruct((t, DN_W), BF16),
                   jax.ShapeDtypeStruct((t, SW_COLS), BF16),
                   jax.ShapeDtypeStruct((t, GATE_COLS), BF16),
                   jax.ShapeDtypeStruct((t, BA_COLS), F32)],
        scratch_shapes=[pltpu.VMEM((8, 3 * DN_W), F32), pltpu.VMEM((3 * DN_HEADS, tm, LANES), F32),
                        pltpu.VMEM((3 * DN_HEADS, tm, LANES), F32)],
        compiler_params=pltpu.CompilerParams(dimension_semantics=("arbitrary",),
                                             vmem_limit_bytes=VMEM_LIMIT),
        name="inproj",
    )(x, g, w, conv_w)


def _deltanet_kernel(qkv_ref, z_ref, ba_ref, alog_ref, dtb_ref, ng_ref, o_ref,
                     state_ref, u_scr, wq_scr, ai_scr, kd_scr, el_scr):
    C = DN_CHUNK
    B = qkv_ref.shape[0]
    s = pl.program_id(0)

    @pl.when(s == 0)
    def _():
        u_scr[...] = jnp.zeros_like(u_scr)
        wq_scr[...] = jnp.zeros_like(wq_scr)
        ai_scr[...] = jnp.zeros_like(ai_scr)
        kd_scr[...] = jnp.zeros_like(kd_scr)
        el_scr[...] = jnp.zeros_like(el_scr)
        state_ref[...] = jnp.zeros_like(state_ref)

    lane = lax.broadcasted_iota(jnp.int32, (C, LANES), 1)
    row = lax.broadcasted_iota(jnp.int32, (C, LANES), 0)
    causal = row >= lane
    strict = row > lane
    eye = jnp.where(row == lane, 1.0, 0.0)
    sib = row ^ lane
    BH = [(b, h) for b in range(B) for h in range(DN_HEADS)]
    N = range(len(BH))

    def lane_bcast(x, j):
        return jnp.broadcast_to(x[:, j:j + 1], (x.shape[0], LANES))

    state = [state_ref[i] for i in N]
    el = [lane_bcast(el_scr[b, 0:1, :], DN_HEADS + h) for b, h in BH]
    ws = [_bdot(wq_scr[i], state[i]) for i in N]

    tril = jnp.where(causal, 1.0, 0.0).astype(BF16)
    gc_all, gc_all_t, beta_all, eg_all, ed_all, el_new = [], [], [], [], [], []
    for b in range(B):
        ba = ba_ref[b]
        xg = ba + dtb_ref[...]
        softplus = jnp.maximum(xg, 0.0) + jnp.log(1.0 + jnp.exp(-jnp.abs(xg)))
        g_all = -jnp.exp(alog_ref[...]) * softplus
        g_hi = g_all.astype(BF16)
        r1 = g_all - g_hi.astype(F32)
        g_mid = r1.astype(BF16)
        g_lo = (r1 - g_mid.astype(F32)).astype(BF16)
        gc = (jnp.dot(tril, g_hi, preferred_element_type=F32)
              + jnp.dot(tril, g_mid, preferred_element_type=F32)
              + jnp.dot(tril, g_lo, preferred_element_type=F32))
        g_last = gc[C - 1:C, :]
        gc_all.append(gc)
        gc_all_t.append(gc.T)
        beta_all.append(_sigmoid(ba).astype(BF16))
        eg_all.append(jnp.exp(gc).astype(BF16))
        ed_all.append(jnp.exp(g_last - gc).astype(BF16))
        el_new.append(jnp.exp(g_last))

    q = [qkv_ref[b, :, h * DN_DK:(h + 1) * DN_DK] for b, h in BH]
    k = [qkv_ref[b, :, DN_W + h * DN_DK:DN_W + (h + 1) * DN_DK] for b, h in BH]
    v = [qkv_ref[b, :, 2 * DN_W + h * DN_DV:2 * DN_W + (h + 1) * DN_DV] for b, h in BH]
    beta = [lane_bcast(beta_all[b], h) for b, h in BH]
    eg = [lane_bcast(eg_all[b], DN_HEADS + h) for b, h in BH]
    ed = [lane_bcast(ed_all[b], DN_HEADS + h) for b, h in BH]
    kb = [k[i] * beta[i] for i in N]
    kk = [_bdot_nt(kb[i], k[i]) for i in N]
    qk = [_bdot_nt(q[i], k[i]) for i in N]

    v_new = [u_scr[i] - ws[i][:C] for i in N]
    av = [_bdot(ai_scr[i], v_new[i]) for i in N]
    kv = [_bdot_tn(kd_scr[i], v_new[i]) for i in N]

    gc = [lane_bcast(gc_all[b], DN_HEADS + h) for b, h in BH]
    gc_t = [jnp.broadcast_to(gc_all_t[b][DN_HEADS + h:DN_HEADS + h + 1, :], (C, LANES))
            for b, h in BH]
    decay = [jnp.where(causal, jnp.exp(jnp.where(causal, gc[i] - gc_t[i], 0.0)), 0.0) for i in N]
    a_mat = [jnp.where(strict, kk[i] * decay[i], 0.0) for i in N]
    a_intra = [jnp.where(causal, qk[i] * decay[i], 0.0).astype(BF16) for i in N]
    t_mat = [eye - jnp.where(sib < 2, a_mat[i], 0.0) for i in N]

    def level(size, t_mat):
        off = (sib >= size) & (sib < 2 * size)
        if size < 8:
            x_mat = [_bdot(jnp.where(off, a_mat[i], 0.0), t_mat[i]) for i in N]
            return [t_mat[i] - _bdot(t_mat[i], x_mat[i]) for i in N]
        odd = [r for r in range(0, C, size) if (r // size) % 2 == 1]
        rows = lambda m: jnp.concatenate([m[r:r + size] for r in odd], axis=0)
        off_rows = rows(jnp.where(off, 1.0, 0.0)) > 0.5
        zero = jnp.zeros((size, C), F32)
        x_half = [_bdot(jnp.where(off_rows, rows(a_mat[i]), 0.0), t_mat[i]) for i in N]
        t_rows = [rows(t_mat[i]) for i in N]
        x_full = [jnp.concatenate(
            [x_half[i][(r // (2 * size)) * size:(r // (2 * size) + 1) * size]
             if r in odd else zero for r in range(0, C, size)], axis=0) for i in N]
        new_rows = [t_rows[i] - _bdot(t_rows[i], x_full[i]) for i in N]
        return [jnp.concatenate(
            [new_rows[i][(r // (2 * size)) * size:(r // (2 * size) + 1) * size]
             if r in odd else t_mat[i][r:r + size] for r in range(0, C, size)], axis=0)
            for i in N]

    t_mat = level(2, t_mat)

    for i, (b, h) in enumerate(BH):
        state_ref[i] = state[i] * el[i] + kv[i]
        o_i = ws[i][C:] + av[i]
        o_ref[b, :, h * DN_DV:(h + 1) * DN_DV] = (
            _rms(o_i, ng_ref[...]) * z_ref[b, :, h * DN_DV:(h + 1) * DN_DV].astype(F32)
        ).astype(o_ref.dtype)

    size = 4
    while size < C:
        t_mat = level(size, t_mat)
        size *= 2
    uw = [_bdot(t_mat[i], jnp.concatenate([v[i] * beta[i], kb[i] * eg[i]], axis=1))
          for i in N]
    for i, (b, h) in enumerate(BH):
        u_scr[i] = uw[i][:, :DN_DV]
        wq_scr[i, 0:C, :] = uw[i][:, DN_DV:].astype(BF16)
        wq_scr[i, C:2 * C, :] = q[i] * eg[i]
        ai_scr[i] = a_intra[i]
        kd_scr[i] = k[i] * ed[i]
    for b in range(B):
        el_scr[b] = jnp.broadcast_to(el_new[b], el_scr.shape[1:])


def _deltanet(qkv, z, ba, alog_row, dtb_row, norm_g, batch, seq):
    C = DN_CHUNK
    nc = seq // C
    n_bh = batch * DN_HEADS
    cur = lambda s: (0, jnp.minimum(s, nc - 1), 0)
    prev = lambda s: (0, jnp.maximum(s - 1, 0), 0)
    const = lambda s: (0, 0)
    out = pl.pallas_call(
        _deltanet_kernel,
        grid=(nc + 1,),
        in_specs=[pl.BlockSpec((batch, C, 3 * DN_W), cur),
                  pl.BlockSpec((batch, C, DN_W), prev),
                  pl.BlockSpec((batch, C, LANES), cur),
                  pl.BlockSpec((1, LANES), const), pl.BlockSpec((1, LANES), const),
                  pl.BlockSpec((1, LANES), const)],
        out_specs=pl.BlockSpec((batch, C, DN_W), prev),
        out_shape=jax.ShapeDtypeStruct((batch, seq, DN_W), BF16),
        scratch_shapes=[pltpu.VMEM((n_bh, DN_DK, DN_DV), F32),
                        pltpu.VMEM((n_bh, C, DN_DV), F32),
                        pltpu.VMEM((n_bh, 2 * C, DN_DK), BF16),
                        pltpu.VMEM((n_bh, C, C), BF16),
                        pltpu.VMEM((n_bh, C, DN_DK), BF16),
                        pltpu.VMEM((batch, 8, LANES), F32)],
        compiler_params=pltpu.CompilerParams(dimension_semantics=("arbitrary",),
                                             vmem_limit_bytes=VMEM_LIMIT),
        name="deltanet",
    )(qkv.reshape(batch, seq, 3 * DN_W), z.reshape(batch, seq, DN_W),
      ba.reshape(batch, seq, LANES), alog_row, dtb_row, norm_g)
    return out.reshape(batch * seq, DN_W)


def _rope_table_kernel(pos_ref, freq_ref, mcos_ref, msin1_ref, msin2_ref,
                       c_ref, s1_ref, s2_ref):
    ang = pos_ref[...].astype(F32) * freq_ref[...]
    cos = jnp.cos(ang)
    sin = jnp.sin(ang)
    c_ref[...] = jnp.where(mcos_ref[...] > 0.5, cos, 1.0)
    s1_ref[...] = jnp.where(msin1_ref[...] > 0.5, -sin, 0.0)
    s2_ref[...] = jnp.where(msin2_ref[...] > 0.5, sin, 0.0)


def _rope_tables(positions, tm=1024):
    t = positions.size
    half = ROT_DIM // 2
    lane = np.arange(LANES)
    within = lane % SW_HEAD_DIM
    inv_freq = ROPE_THETA ** (-np.arange(half, dtype=np.float32) * (2.0 / ROT_DIM))
    freq = np.where(within < ROT_DIM, inv_freq[within % half], 0.0).astype(np.float32)
    mcos = (within < ROT_DIM).astype(np.float32)
    msin1 = (within < half).astype(np.float32)
    msin2 = ((within >= half) & (within < ROT_DIM)).astype(np.float32)
    rows = [jnp.asarray(a.reshape(1, LANES)) for a in (freq, mcos, msin1, msin2)]
    const = lambda i: (0, 0)
    row = lambda i: (i, 0)
    out = jax.ShapeDtypeStruct((t, LANES), F32)
    return pl.pallas_call(
        _rope_table_kernel,
        grid=(t // tm,),
        in_specs=[pl.BlockSpec((tm, 1), row)] + [pl.BlockSpec((1, LANES), const)] * 4,
        out_specs=[pl.BlockSpec((tm, LANES), row)] * 3,
        out_shape=[out, out, out],
        name="rope_tables",
    )(positions.reshape(t, 1), *rows)


def _swa_kernel(sink_ref, q_ref, kv_ref, c_ref, s1_ref, s2_ref, o_ref, k_scr, v_scr):
    Q = SW_BLOCK
    n = pl.program_id(1)
    half = SW_HEAD_DIM
    cos = c_ref[...]
    sin1 = s1_ref[...]
    sin2 = s2_ref[...]
    lane = lax.broadcasted_iota(jnp.int32, (Q, LANES), 1)
    low = lane < half
    log2e = 1.4426950408889634

    def rope(x):
        return x * cos + pltpu.roll(x, LANES - ROT_DIM // 2, 1) * sin1 \
            + pltpu.roll(x, ROT_DIM // 2, 1) * sin2

    @pl.when(n == 0)
    def _():
        k_scr[...] = jnp.zeros_like(k_scr)
        v_scr[...] = jnp.zeros_like(v_scr)

    @pl.when(n > 0)
    def _():
        for j in range(SW_KV_HEADS):
            k_scr[j, 0:Q, :] = k_scr[j, Q:2 * Q, :]
            v_scr[2 * j, 0:Q, :] = v_scr[2 * j, Q:2 * Q, :]
            v_scr[2 * j + 1, 0:Q, :] = v_scr[2 * j + 1, Q:2 * Q, :]

    kv = kv_ref[...].astype(F32)
    kr = rope(kv[:, :LANES])
    kr_sw = pltpu.roll(kr, half, 1)
    vt = kv[:, LANES:]
    vt_sw = pltpu.roll(vt, half, 1)
    ones_lo = jnp.where(lane == half, 1.0, 0.0)
    ones_hi = jnp.where(lane == 0, 1.0, 0.0)
    k_scr[0, Q:2 * Q, :] = jnp.where(low, kr, kr_sw).astype(BF16)
    k_scr[1, Q:2 * Q, :] = jnp.where(low, kr_sw, kr).astype(BF16)
    v_scr[0, Q:2 * Q, :] = jnp.where(low, vt, ones_lo).astype(BF16)
    v_scr[1, Q:2 * Q, :] = jnp.where(low, ones_hi, vt_sw).astype(BF16)
    v_scr[2, Q:2 * Q, :] = jnp.where(low, vt_sw, ones_lo).astype(BF16)
    v_scr[3, Q:2 * Q, :] = jnp.where(low, ones_hi, vt).astype(BF16)

    qi = lax.broadcasted_iota(jnp.int32, (Q, 2 * Q), 0)
    ki = lax.broadcasted_iota(jnp.int32, (Q, 2 * Q), 1)
    valid = (ki > qi) & (ki <= qi + Q) & ((ki >= Q) | (n > 0))

    pairs_per_kv = SW_Q_HEADS // SW_KV_HEADS // 2
    for p0 in range(0, SW_Q_HEADS // 2, SW_PAIR_GROUP):
        heads = []
        for pair in range(p0, p0 + SW_PAIR_GROUP):
            heads += [(pair, pair // pairs_per_kv, False), (pair, pair // pairs_per_kv, True)]
        qp = {pair: rope(q_ref[:, pair * LANES:(pair + 1) * LANES].astype(F32))
              * (SW_HEAD_DIM ** -0.5 * log2e) for pair in range(p0, p0 + SW_PAIR_GROUP)}
        s = [_bdot_nt(jnp.where(low, 0.0, qp[pair]) if hi else jnp.where(low, qp[pair], 0.0),
                      k_scr[j]) for pair, j, hi in heads]
        sink = [sink_ref[2 * pair + int(hi)] * log2e for pair, j, hi in heads]
        s = [jnp.where(valid, t, -jnp.inf) for t in s]
        m = [jnp.maximum(jnp.max(s[i], axis=-1, keepdims=True), sink[i])
             for i in range(len(heads))]
        p = [jnp.exp2(s[i] - m[i]) for i in range(len(heads))]
        pv = [_bdot(p[i], v_scr[2 * j + int(hi)]) for i, (pair, j, hi) in enumerate(heads)]
        out = []
        for i, (pair, j, hi) in enumerate(heads):
            sum_lane = 0 if hi else half
            denom = pv[i][:, sum_lane:sum_lane + 1] + jnp.exp2(sink[i] - m[i])
            out.append(pv[i] * (1.0 / denom))
        for i in range(0, len(heads), 2):
            pair = heads[i][0]
            o_ref[:, pair * LANES:(pair + 1) * LANES] = jnp.where(
                low, out[i], out[i + 1]).astype(o_ref.dtype)


def _swa(sw, sinks, cos_t, sin1_t, sin2_t, batch, seq):
    t = sw.shape[0]
    Q = SW_BLOCK
    nb = seq // Q
    blk = lambda b, n, s: (b * nb + n, 0)
    return pl.pallas_call(
        _swa_kernel,
        grid_spec=pltpu.PrefetchScalarGridSpec(
            num_scalar_prefetch=1,
            grid=(batch, nb),
            in_specs=[pl.BlockSpec((Q, SW_Q_W), blk),
                      pl.BlockSpec((Q, 2 * SW_KV_W), lambda b, n, s: (b * nb + n, SW_Q_W // (2 * SW_KV_W))),
                      pl.BlockSpec((Q, LANES), blk), pl.BlockSpec((Q, LANES), blk),
                      pl.BlockSpec((Q, LANES), blk)],
            out_specs=pl.BlockSpec((Q, SW_Q_W), blk),
            scratch_shapes=[pltpu.VMEM((SW_KV_HEADS, 2 * Q, LANES), BF16),
                            pltpu.VMEM((2 * SW_KV_HEADS, 2 * Q, LANES), BF16)]),
        out_shape=jax.ShapeDtypeStruct((t, SW_Q_W), BF16),
        compiler_params=pltpu.CompilerParams(dimension_semantics=("arbitrary", "arbitrary"),
                                             vmem_limit_bytes=VMEM_LIMIT),
        name="swa",
    )(sinks, sw, sw, cos_t, sin1_t, sin2_t)


def _merge_kernel(x_ref, odn_ref, osw_ref, gate_ref, wdn_ref, wsw_ref, wo_ref, g_ref, out_ref):
    y_a = jnp.dot(odn_ref[...], wdn_ref[...], preferred_element_type=F32)
    y_b = jnp.dot(osw_ref[...], wsw_ref[...], preferred_element_type=F32)
    ga = gate_ref[:, :D_MODEL].astype(F32)
    gb = gate_ref[:, D_MODEL:].astype(F32)
    mix = _sigmoid(ga) * y_a + _sigmoid(gb) * y_b
    y = jnp.dot(mix.astype(BF16), wo_ref[...], preferred_element_type=F32)
    out_ref[...] = x_ref[...] + _rms(y, g_ref[...])


def _merge(x, o_dn, o_sw, gates, w_dn, w_sw, w_o, layer, g, tm=512):
    t = x.shape[0]
    row = lambda i: (i, 0)
    const = lambda i: (0, 0)
    wspec = pl.BlockSpec((None, D_MODEL, D_MODEL), lambda i: (layer, 0, 0))
    return pl.pallas_call(
        _merge_kernel,
        grid=(t // tm,),
        in_specs=[pl.BlockSpec((tm, D_MODEL), row), pl.BlockSpec((tm, DN_W), row),
                  pl.BlockSpec((tm, SW_Q_W), row), pl.BlockSpec((tm, GATE_COLS), row),
                  wspec, wspec, wspec, pl.BlockSpec((1, D_MODEL), const)],
        out_specs=pl.BlockSpec((tm, D_MODEL), row),
        out_shape=jax.ShapeDtypeStruct((t, D_MODEL), F32),
        compiler_params=pltpu.CompilerParams(dimension_semantics=("arbitrary",),
                                             vmem_limit_bytes=VMEM_LIMIT),
        name="merge",
    )(x, o_dn, o_sw, gates, w_dn, w_sw, w_o, g)


def _mlp_kernel(x_ref, gpre_ref, w1_ref, w2_ref, gpost_ref, out_ref, h_scr, acc_scr):
    f = pl.program_id(1)

    @pl.when(f == 0)
    def _():
        h_scr[...] = _rms(x_ref[...], gpre_ref[...]).astype(BF16)
        acc_scr[...] = jnp.zeros_like(acc_scr)

    a = jnp.maximum(jnp.dot(h_scr[...], w1_ref[...], preferred_element_type=F32), 0.0)
    acc_scr[...] += jnp.dot((a * a).astype(BF16), w2_ref[...], preferred_element_type=F32)

    @pl.when(f == pl.num_programs(1) - 1)
    def _():
        out_ref[...] = x_ref[...] + _rms(acc_scr[...], gpost_ref[...])


def _mlp(x, g_pre, w1, w2, layer, g_post, tm=1024, tf=1024):
    t = x.shape[0]
    row = lambda i, f: (i, 0)
    const = lambda i, f: (0, 0)
    return pl.pallas_call(
        _mlp_kernel,
        grid=(t // tm, D_FF // tf),
        in_specs=[pl.BlockSpec((tm, D_MODEL), row), pl.BlockSpec((1, D_MODEL), const),
                  pl.BlockSpec((None, D_MODEL, tf), lambda i, f: (layer, 0, f)),
                  pl.BlockSpec((None, tf, D_MODEL), lambda i, f: (layer, f, 0)),
                  pl.BlockSpec((1, D_MODEL), const)],
        out_specs=pl.BlockSpec((tm, D_MODEL), row),
        out_shape=jax.ShapeDtypeStruct((t, D_MODEL), F32),
        scratch_shapes=[pltpu.VMEM((tm, D_MODEL), BF16), pltpu.VMEM((tm, D_MODEL), F32)],
        compiler_params=pltpu.CompilerParams(dimension_semantics=("arbitrary", "arbitrary"),
                                             vmem_limit_bytes=VMEM_LIMIT),
        name="mlp",
    )(x, g_pre, w1, w2, g_post)


def _pack_w_in(w_in):
    o = 0
    dn = w_in[..., o:o + DN_COLS]
    o += DN_COLS
    ba = w_in[..., o:o + 2 * DN_HEADS]
    o += 2 * DN_HEADS
    sw = w_in[..., o:o + SW_COLS]
    o += SW_COLS
    gates = w_in[..., o:o + GATE_COLS]
    ba = jnp.pad(ba, [(0, 0)] * (ba.ndim - 1) + [(0, BA_COLS - 2 * DN_HEADS)])
    return jnp.concatenate([dn, sw, gates, ba], axis=-1).astype(BF16)


def _lane_row(vec, offset):
    return jnp.zeros((1, LANES), F32).at[0, offset:offset + vec.shape[0]].set(vec.astype(F32))


def kernel(x, positions, pre_mix_g, w_in, dn_conv_w, dn_a_log, dn_dt_bias, dn_norm_g, sw_sinks,
           w_up_dn, w_up_sw, w_o, post_mix_g, pre_mlp_g, w_ff1, w_ff2, post_mlp_g):
    batch, seq, _ = x.shape
    depth = w_in.shape[0]
    t = batch * seq
    xf = x.reshape(t, D_MODEL)
    cos_t, sin1_t, sin2_t = _rope_tables(positions)
    w_in_p = _pack_w_in(w_in)
    w_dn, w_sw, w_out = w_up_dn.astype(BF16), w_up_sw.astype(BF16), w_o.astype(BF16)
    w1, w2 = w_ff1.astype(BF16), w_ff2.astype(BF16)
    for l in range(depth):
        qkv, z, sw, gates, ba = _inproj(xf, pre_mix_g[l].reshape(1, D_MODEL), w_in_p, l,
                                        dn_conv_w[l], seq)
        o_dn = _deltanet(qkv, z, ba, _lane_row(dn_a_log[l], DN_HEADS),
                         _lane_row(dn_dt_bias[l], DN_HEADS), dn_norm_g[l].reshape(1, DN_DV),
                         batch, seq)
        o_sw = _swa(sw, sw_sinks[l].astype(F32), cos_t, sin1_t, sin2_t, batch, seq)
        xf = _merge(xf, o_dn, o_sw, gates, w_dn, w_sw, w_out, l,
                    post_mix_g[l].reshape(1, D_MODEL))
        xf = _mlp(xf, pre_mlp_g[l].reshape(1, D_MODEL), w1, w2, l,
                  post_mlp_g[l].reshape(1, D_MODEL))
    return xf.reshape(batch, seq, D_MODEL)
```

```python
import functools

import numpy as np
import jax
import jax.numpy as jnp
from jax import lax
from jax.experimental import pallas as pl
from jax.experimental.pallas import tpu as pltpu

F32 = jnp.float32
BF16 = jnp.bfloat16

D_MODEL = 1024
DN_HEADS = 8
DN_DK = 128
DN_DV = 128
DN_CONV = 4
SW_Q_HEADS = 16
SW_KV_HEADS = 2
SW_HEAD_DIM = 64
SW_BLOCK = 128
ROPE_THETA = 500000.0
ROT_DIM = SW_HEAD_DIM // 4
D_FF = 4 * D_MODEL
EPS = 1e-6

DN_W = DN_HEADS * DN_DK
SW_Q_W = SW_Q_HEADS * SW_HEAD_DIM
SW_KV_W = SW_KV_HEADS * SW_HEAD_DIM
LANES = 128

DN_COLS = 4 * DN_W
SW_COLS = SW_Q_W + 2 * SW_KV_W
GATE_COLS = 2 * D_MODEL
BA_COLS = LANES
PACKED_COLS = DN_COLS + SW_COLS + GATE_COLS + BA_COLS

DN_CHUNK = 128
MLP_SUB = 256
MLP_FF_CHUNK = 1024
MERGE_SUB = 256
INPROJ_CHUNK = 256
VMEM_LIMIT = 56 * 1024 * 1024


def _bdot(a, b):
    return jnp.dot(a.astype(BF16), b.astype(BF16), preferred_element_type=F32)


def _bdot_nt(a, b):
    return lax.dot_general(a.astype(BF16), b.astype(BF16), (((1,), (1,)), ((), ())),
                           preferred_element_type=F32)


def _bdot_tn(a, b):
    return lax.dot_general(a.astype(BF16), b.astype(BF16), (((0,), (0,)), ((), ())),
                           preferred_element_type=F32)


def _rms(x, g):
    return x * lax.rsqrt(jnp.mean(x * x, axis=-1, keepdims=True) + EPS) * g


def _sigmoid(x):
    return 1.0 / (1.0 + jnp.exp(-x))


def _inproj_kernel(x_ref, g_ref, w_ref, cw_ref, qkv_ref, z_ref, sw_ref, gate_ref, ba_ref,
                   tail_ref, pbuf_ref, ybuf_ref, *, tiles_per_seq):
    tm = x_ref.shape[0]

    @pl.when(pl.program_id(0) % tiles_per_seq == 0)
    def _():
        tail_ref[...] = jnp.zeros_like(tail_ref)

    h = _rms(x_ref[...], g_ref[...]).astype(BF16)

    def proj(off, width):
        return jnp.dot(h, w_ref[:, off:off + width], preferred_element_type=F32)

    def conv_matmul(c):
        p = proj(c, INPROJ_CHUNK)
        for hh in range(0, INPROJ_CHUNK, DN_DK):
            pbuf_ref[(c + hh) // DN_DK] = p[:, hh:hh + DN_DK]

    def conv_epilogue(c):
        rows = tm // 8
        for hh in range(0, INPROJ_CHUNK, DN_DK):
            g = (c + hh) // DN_DK
            cols = slice(c + hh, c + hh + DN_DK)
            x = [pbuf_ref[g, pl.ds(j, rows, stride=8), :] for j in range(8)]
            prev = {}
            for j in range(8 - (DN_CONV - 1), 8):
                prev[j - 8] = jnp.concatenate([tail_ref[j:j + 1, cols], x[j][:rows - 1, :]],
                                              axis=0)
            tail_ref[:, cols] = pbuf_ref[g, tm - 8:tm, :]
            tok = lambda j: x[j] if j >= 0 else prev[j]
            for j in range(8):
                y = x[j] * cw_ref[DN_CONV - 1:DN_CONV, cols]
                for i in range(DN_CONV - 1):
                    y = y + tok(j - (DN_CONV - 1 - i)) * cw_ref[i:i + 1, cols]
                y = y * _sigmoid(y)
                if c + hh < 2 * DN_W:
                    scale = DN_DK ** -0.5 if c + hh < DN_W else 1.0
                    y = y * (lax.rsqrt(jnp.sum(y * y, axis=-1, keepdims=True) + EPS) * scale)
                ybuf_ref[g, pl.ds(j, rows, stride=8), :] = y
            qkv_ref[:, cols] = ybuf_ref[g].astype(qkv_ref.dtype)

    def silu_chunk(c):
        zc = proj(3 * DN_W + c, INPROJ_CHUNK)
        z_ref[:, c:c + INPROJ_CHUNK] = (zc * _sigmoid(zc)).astype(z_ref.dtype)

    def plain_chunk(ref, off, c, w):
        ref[:, c:c + w] = proj(off + c, w).astype(ref.dtype)

    conv_cols = list(range(0, 3 * DN_W, INPROJ_CHUNK))
    light = [functools.partial(silu_chunk, c) for c in range(0, DN_W, INPROJ_CHUNK)]
    off = DN_COLS
    for ref, width in ((sw_ref, SW_COLS), (gate_ref, GATE_COLS), (ba_ref, BA_COLS)):
        for c in range(0, width, INPROJ_CHUNK):
            light.append(functools.partial(plain_chunk, ref, off, c, min(INPROJ_CHUNK, width - c)))
        off += width
    per_conv = -(-len(light) // len(conv_cols))
    conv_matmul(conv_cols[0])
    for i, c in enumerate(conv_cols):
        for fn in light[i * per_conv:(i + 1) * per_conv]:
            fn()
        if i + 1 < len(conv_cols):
            conv_matmul(conv_cols[i + 1])
        conv_epilogue(c)


def _inproj(x, g, w, layer, conv_w, seq, tm=512):
    t = x.shape[0]
    row = lambda i: (i, 0)
    const = lambda i: (0, 0)
    return pl.pallas_call(
        functools.partial(_inproj_kernel, tiles_per_seq=seq // tm),
        grid=(t // tm,),
        in_specs=[pl.BlockSpec((tm, D_MODEL), row),
                  pl.BlockSpec((1, D_MODEL), const),
                  pl.BlockSpec((None, D_MODEL, PACKED_COLS), lambda i: (layer, 0, 0),
                               pipeline_mode=pl.Buffered(1)),
                  pl.BlockSpec((DN_CONV, 3 * DN_W), const)],
        out_specs=[pl.BlockSpec((tm, 3 * DN_W), row), pl.BlockSpec((tm, DN_W), row),
                   pl.BlockSpec((tm, SW_COLS), row),
                   pl.BlockSpec((tm, GATE_COLS), row), pl.BlockSpec((tm, BA_COLS), row)],
        out_shape=[jax.ShapeDtypeStruct((t, 3 * DN_W), BF16),
                   jax.ShapeDtypeStruct((t, DN_W), BF16),
                   jax.ShapeDtypeStruct((t, SW_COLS), BF16),
                   jax.ShapeDtypeStruct((t, GATE_COLS), BF16),
                   jax.ShapeDtypeStruct((t, BA_COLS), F32)],
        scratch_shapes=[pltpu.VMEM((8, 3 * DN_W), F32), pltpu.VMEM((3 * DN_HEADS, tm, LANES), F32),
                        pltpu.VMEM((3 * DN_HEADS, tm, LANES), F32)],
        compiler_params=pltpu.CompilerParams(dimension_semantics=("arbitrary",),
                                             vmem_limit_bytes=VMEM_LIMIT),
        name="inproj",
    )(x, g, w, conv_w)


def _deltanet_kernel(qkv_ref, z_ref, ba_ref, alog_ref, dtb_ref, ng_ref, o_ref,
                     state_ref, u_scr, wq_scr, ai_scr, kd_scr, el_scr):
    C = DN_CHUNK
    B = qkv_ref.shape[0]
    s = pl.program_id(0)

    @pl.when(s == 0)
    def _():
        u_scr[...] = jnp.zeros_like(u_scr)
        wq_scr[...] = jnp.zeros_like(wq_scr)
        ai_scr[...] = jnp.zeros_like(ai_scr)
        kd_scr[...] = jnp.zeros_like(kd_scr)
        el_scr[...] = jnp.zeros_like(el_scr)
        state_ref[...] = jnp.zeros_like(state_ref)

    lane = lax.broadcasted_iota(jnp.int32, (C, LANES), 1)
    row = lax.broadcasted_iota(jnp.int32, (C, LANES), 0)
    causal = row >= lane
    strict = row > lane
    eye = jnp.where(row == lane, 1.0, 0.0)
    sib = row ^ lane
    BH = [(b, h) for b in range(B) for h in range(DN_HEADS)]
    N = range(len(BH))

    def lane_bcast(x, j):
        return jnp.broadcast_to(x[:, j:j + 1], (x.shape[0], LANES))

    state = [state_ref[i] for i in N]
    el = [lane_bcast(el_scr[b, 0:1, :], DN_HEADS + h) for b, h in BH]
    ws = [_bdot(wq_scr[i], state[i]) for i in N]

    tril = jnp.where(causal, 1.0, 0.0).astype(BF16)
    gc_all, gc_all_t, beta_all, eg_all, ed_all, el_new = [], [], [], [], [], []
    for b in range(B):
        ba = ba_ref[b]
        xg = ba + dtb_ref[...]
        softplus = jnp.maximum(xg, 0.0) + jnp.log(1.0 + jnp.exp(-jnp.abs(xg)))
        g_all = -jnp.exp(alog_ref[...]) * softplus
        g_hi = g_all.astype(BF16)
        r1 = g_all - g_hi.astype(F32)
        g_mid = r1.astype(BF16)
        g_lo = (r1 - g_mid.astype(F32)).astype(BF16)
        gc = (jnp.dot(tril, g_hi, preferred_element_type=F32)
              + jnp.dot(tril, g_mid, preferred_element_type=F32)
              + jnp.dot(tril, g_lo, preferred_element_type=F32))
        g_last = gc[C - 1:C, :]
        gc_all.append(gc)
        gc_all_t.append(gc.T)
        beta_all.append(_sigmoid(ba).astype(BF16))
        eg_all.append(jnp.exp(gc).astype(BF16))
        ed_all.append(jnp.exp(g_last - gc).astype(BF16))
        el_new.append(jnp.exp(g_last))

    q = [qkv_ref[b, :, h * DN_DK:(h + 1) * DN_DK] for b, h in BH]
    k = [qkv_ref[b, :, DN_W + h * DN_DK:DN_W + (h + 1) * DN_DK] for b, h in BH]
    v = [qkv_ref[b, :, 2 * DN_W + h * DN_DV:2 * DN_W + (h + 1) * DN_DV] for b, h in BH]
    beta = [lane_bcast(beta_all[b], h) for b, h in BH]
    eg = [lane_bcast(eg_all[b], DN_HEADS + h) for b, h in BH]
    ed = [lane_bcast(ed_all[b], DN_HEADS + h) for b, h in BH]
    kb = [k[i] * beta[i] for i in N]
    kk = [_bdot_nt(kb[i], k[i]) for i in N]
    qk = [_bdot_nt(q[i], k[i]) for i in N]

    v_new = [u_scr[i] - ws[i][:C] for i in N]
    av = [_bdot(ai_scr[i], v_new[i]) for i in N]
    kv = [_bdot_tn(kd_scr[i], v_new[i]) for i in N]

    gc = [lane_bcast(gc_all[b], DN_HEADS + h) for b, h in BH]
    gc_t = [jnp.broadcast_to(gc_all_t[b][DN_HEADS + h:DN_HEADS + h + 1, :], (C, LANES))
            for b, h in BH]
    decay = [jnp.where(causal, jnp.exp(jnp.where(causal, gc[i] - gc_t[i], 0.0)), 0.0) for i in N]
    a_mat = [jnp.where(strict, kk[i] * decay[i], 0.0) for i in N]
    a_intra = [jnp.where(causal, qk[i] * decay[i], 0.0).astype(BF16) for i in N]
    t_mat = [eye - jnp.where(sib < 2, a_mat[i], 0.0) for i in N]

    def level(size, t_mat):
        off = (sib >= size) & (sib < 2 * size)
        if size < 8:
            x_mat = [_bdot(jnp.where(off, a_mat[i], 0.0), t_mat[i]) for i in N]
            return [t_mat[i] - _bdot(t_mat[i], x_mat[i]) for i in N]
        odd = [r for r in range(0, C, size) if (r // size) % 2 == 1]
        rows = lambda m: jnp.concatenate([m[r:r + size] for r in odd], axis=0)
        off_rows = rows(jnp.where(off, 1.0, 0.0)) > 0.5
        zero = jnp.zeros((size, C), F32)
        x_half = [_bdot(jnp.where(off_rows, rows(a_mat[i]), 0.0), t_mat[i]) for i in N]
        t_rows = [rows(t_mat[i]) for i in N]
        x_full = [jnp.concatenate(
            [x_half[i][(r // (2 * size)) * size:(r // (2 * size) + 1) * size]
             if r in odd else zero for r in range(0, C, size)], axis=0) for i in N]
        new_rows = [t_rows[i] - _bdot(t_rows[i], x_full[i]) for i in N]
        return [jnp.concatenate(
            [new_rows[i][(r // (2 * size)) * size:(r // (2 * size) + 1) * size]
             if r in odd else t_mat[i][r:r + size] for r in range(0, C, size)], axis=0)
            for i in N]

    t_mat = level(2, t_mat)

    for i, (b, h) in enumerate(BH):
        state_ref[i] = state[i] * el[i] + kv[i]
        o_i = ws[i][C:] + av[i]
        o_ref[b, :, h * DN_DV:(h + 1) * DN_DV] = (
            _rms(o_i, ng_ref[...]) * z_ref[b, :, h * DN_DV:(h + 1) * DN_DV].astype(F32)
        ).astype(o_ref.dtype)

    size = 4
    while size < C:
        t_mat = level(size, t_mat)
        size *= 2
    uw = [_bdot(t_mat[i], jnp.concatenate([v[i] * beta[i], kb[i] * eg[i]], axis=1))
          for i in N]
    for i, (b, h) in enumerate(BH):
        u_scr[i] = uw[i][:, :DN_DV]
        wq_scr[i, 0:C, :] = uw[i][:, DN_DV:].astype(BF16)
        wq_scr[i, C:2 * C, :] = q[i] * eg[i]
        ai_scr[i] = a_intra[i]
        kd_scr[i] = k[i] * ed[i]
    for b in range(B):
        el_scr[b] = jnp.broadcast_to(el_new[b], el_scr.shape[1:])


def _deltanet(qkv, z, ba, alog_row, dtb_row, norm_g, batch, seq):
    C = DN_CHUNK
    nc = seq // C
    n_bh = batch * DN_HEADS
    cur = lambda s: (0, jnp.minimum(s, nc - 1), 0)
    prev = lambda s: (0, jnp.maximum(s - 1, 0), 0)
    const = lambda s: (0, 0)
    out = pl.pallas_call(
        _deltanet_kernel,
        grid=(nc + 1,),
        in_specs=[pl.BlockSpec((batch, C, 3 * DN_W), cur),
                  pl.BlockSpec((batch, C, DN_W), prev),
                  pl.BlockSpec((batch, C, LANES), cur),
                  pl.BlockSpec((1, LANES), const), pl.BlockSpec((1, LANES), const),
                  pl.BlockSpec((1, LANES), const)],
        out_specs=pl.BlockSpec((batch, C, DN_W), prev),
        out_shape=jax.ShapeDtypeStruct((batch, seq, DN_W), BF16),
        scratch_shapes=[pltpu.VMEM((n_bh, DN_DK, DN_DV), F32),
                        pltpu.VMEM((n_bh, C, DN_DV), F32),
                        pltpu.VMEM((n_bh, 2 * C, DN_DK), BF16),
                        pltpu.VMEM((n_bh, C, C), BF16),
                        pltpu.VMEM((n_bh, C, DN_DK), BF16),
                        pltpu.VMEM((batch, 8, LANES), F32)],
        compiler_params=pltpu.CompilerParams(dimension_semantics=("arbitrary",),
                                             vmem_limit_bytes=VMEM_LIMIT),
        name="deltanet",
    )(qkv.reshape(batch, seq, 3 * DN_W), z.reshape(batch, seq, DN_W),
      ba.reshape(batch, seq, LANES), alog_row, dtb_row, norm_g)
    return out.reshape(batch * seq, DN_W)


def _rope_table_kernel(pos_ref, freq_ref, mcos_ref, msin1_ref, msin2_ref,
                       c_ref, s1_ref, s2_ref):
    ang = pos_ref[...].astype(F32) * freq_ref[...]
    cos = jnp.cos(ang)
    sin = jnp.sin(ang)
    c_ref[...] = jnp.where(mcos_ref[...] > 0.5, cos, 1.0)
    s1_ref[...] = jnp.where(msin1_ref[...] > 0.5, -sin, 0.0)
    s2_ref[...] = jnp.where(msin2_ref[...] > 0.5, sin, 0.0)


def _rope_tables(positions, tm=1024):
    t = positions.size
    half = ROT_DIM // 2
    lane = np.arange(LANES)
    within = lane % SW_HEAD_DIM
    inv_freq = ROPE_THETA ** (-np.arange(half, dtype=np.float32) * (2.0 / ROT_DIM))
    freq = np.where(within < ROT_DIM, inv_freq[within % half], 0.0).astype(np.float32)
    mcos = (within < ROT_DIM).astype(np.float32)
    msin1 = (within < half).astype(np.float32)
    msin2 = ((within >= half) & (within < ROT_DIM)).astype(np.float32)
    rows = [jnp.asarray(a.reshape(1, LANES)) for a in (freq, mcos, msin1, msin2)]
    const = lambda i: (0, 0)
    row = lambda i: (i, 0)
    out = jax.ShapeDtypeStruct((t, LANES), F32)
    return pl.pallas_call(
        _rope_table_kernel,
        grid=(t // tm,),
        in_specs=[pl.BlockSpec((tm, 1), row)] + [pl.BlockSpec((1, LANES), const)] * 4,
        out_specs=[pl.BlockSpec((tm, LANES), row)] * 3,
        out_shape=[out, out, out],
        name="rope_tables",
    )(positions.reshape(t, 1), *rows)


def _swa_kernel(sink_ref, q_ref, kv_ref, kvn_ref, c_ref, s1_ref, s2_ref, cn_ref, s1n_ref, s2n_ref,
                o_ref, k_scr, v_scr, *, blocks_per_seq):
    Q = SW_BLOCK
    step = pl.program_id(0)
    half = SW_HEAD_DIM
    lane = lax.broadcasted_iota(jnp.int32, (Q, LANES), 1)
    low = lane < half
    log2e = 1.4426950408889634

    def rope(x, cos, sin1, sin2):
        return x * cos + pltpu.roll(x, LANES - ROT_DIM // 2, 1) * sin1 \
            + pltpu.roll(x, ROT_DIM // 2, 1) * sin2

    def prepare(kv_tile, cos, sin1, sin2, dst):
        kv = kv_tile.astype(F32)
        kr = rope(kv[:, :LANES], cos, sin1, sin2)
        kr_sw = pltpu.roll(kr, half, 1)
        vt = kv[:, LANES:]
        vt_sw = pltpu.roll(vt, half, 1)
        ones_lo = jnp.where(lane == half, 1.0, 0.0)
        ones_hi = jnp.where(lane == 0, 1.0, 0.0)
        k_scr[0, dst, :] = jnp.where(low, kr, kr_sw).astype(BF16)
        k_scr[1, dst, :] = jnp.where(low, kr_sw, kr).astype(BF16)
        v_scr[0, dst, :] = jnp.where(low, vt, ones_lo).astype(BF16)
        v_scr[1, dst, :] = jnp.where(low, ones_hi, vt_sw).astype(BF16)
        v_scr[2, dst, :] = jnp.where(low, vt_sw, ones_lo).astype(BF16)
        v_scr[3, dst, :] = jnp.where(low, ones_hi, vt).astype(BF16)

    @pl.when(step == 0)
    def _():
        k_scr[...] = jnp.zeros_like(k_scr)
        v_scr[...] = jnp.zeros_like(v_scr)
        prepare(kv_ref[...], c_ref[...], s1_ref[...], s2_ref[...], slice(Q, 2 * Q))

    @pl.when(step > 0)
    def _():
        for j in range(SW_KV_HEADS):
            k_scr[j, 0:2 * Q, :] = k_scr[j, Q:3 * Q, :]
        for j in range(2 * SW_KV_HEADS):
            v_scr[j, 0:2 * Q, :] = v_scr[j, Q:3 * Q, :]

    first_of_seq = (step % blocks_per_seq) == 0
    qi = lax.broadcasted_iota(jnp.int32, (Q, 2 * Q), 0)
    ki = lax.broadcasted_iota(jnp.int32, (Q, 2 * Q), 1)
    valid = (ki > qi) & (ki <= qi + Q) & ((ki >= Q) | jnp.logical_not(first_of_seq))
    cos = c_ref[...]
    sin1 = s1_ref[...]
    sin2 = s2_ref[...]

    pairs_per_kv = SW_Q_HEADS // SW_KV_HEADS // 2
    heads = []
    for pair in range(SW_Q_HEADS // 2):
        heads += [(pair, pair // pairs_per_kv, False), (pair, pair // pairs_per_kv, True)]
    qp = [rope(q_ref[:, pair * LANES:(pair + 1) * LANES].astype(F32), cos, sin1, sin2)
          * (SW_HEAD_DIM ** -0.5 * log2e) for pair in range(SW_Q_HEADS // 2)]
    s = [_bdot_nt(jnp.where(low, 0.0, qp[pair]) if hi else jnp.where(low, qp[pair], 0.0),
                  k_scr[j, 0:2 * Q, :]) for pair, j, hi in heads]
    sink = [sink_ref[2 * pair + int(hi)] * log2e for pair, j, hi in heads]
    s = [jnp.where(valid, t, -jnp.inf) for t in s]
    m = [jnp.maximum(jnp.max(s[i], axis=-1, keepdims=True), sink[i]) for i in range(len(heads))]
    p = [jnp.exp2(s[i] - m[i]) for i in range(len(heads))]
    pv = [_bdot(p[i], v_scr[2 * j + int(hi), 0:2 * Q, :]) for i, (pair, j, hi) in enumerate(heads)]
    prepare(kvn_ref[...], cn_ref[...], s1n_ref[...], s2n_ref[...], slice(2 * Q, 3 * Q))
    out = []
    for i, (pair, j, hi) in enumerate(heads):
        sum_lane = 0 if hi else half
        denom = pv[i][:, sum_lane:sum_lane + 1] + jnp.exp2(sink[i] - m[i])
        out.append(pv[i] * (1.0 / denom))
    for i in range(0, len(heads), 2):
        pair = heads[i][0]
        o_ref[:, pair * LANES:(pair + 1) * LANES] = jnp.where(
            low, out[i], out[i + 1]).astype(o_ref.dtype)


def _swa(sw, sinks, cos_t, sin1_t, sin2_t, batch, seq):
    t = sw.shape[0]
    Q = SW_BLOCK
    total = t // Q
    blk = lambda i, s: (i, 0)
    nxt = lambda i, s: (jnp.minimum(i + 1, total - 1), 0)
    kv_col = SW_Q_W // (2 * SW_KV_W)
    return pl.pallas_call(
        functools.partial(_swa_kernel, blocks_per_seq=seq // Q),
        grid_spec=pltpu.PrefetchScalarGridSpec(
            num_scalar_prefetch=1,
            grid=(total,),
            in_specs=[pl.BlockSpec((Q, SW_Q_W), blk),
                      pl.BlockSpec((Q, 2 * SW_KV_W), lambda i, s: (i, kv_col)),
                      pl.BlockSpec((Q, 2 * SW_KV_W),
                                   lambda i, s: (jnp.minimum(i + 1, total - 1), kv_col)),
                      pl.BlockSpec((Q, LANES), blk), pl.BlockSpec((Q, LANES), blk),
                      pl.BlockSpec((Q, LANES), blk),
                      pl.BlockSpec((Q, LANES), nxt), pl.BlockSpec((Q, LANES), nxt),
                      pl.BlockSpec((Q, LANES), nxt)],
            out_specs=pl.BlockSpec((Q, SW_Q_W), blk),
            scratch_shapes=[pltpu.VMEM((SW_KV_HEADS, 3 * Q, LANES), BF16),
                            pltpu.VMEM((2 * SW_KV_HEADS, 3 * Q, LANES), BF16)]),
        out_shape=jax.ShapeDtypeStruct((t, SW_Q_W), BF16),
        compiler_params=pltpu.CompilerParams(dimension_semantics=("arbitrary",),
                                             vmem_limit_bytes=VMEM_LIMIT),
        name="swa",
    )(sinks, sw, sw, sw, cos_t, sin1_t, sin2_t, cos_t, sin1_t, sin2_t)


def _merge_kernel(x_ref, odn_ref, osw_ref, gate_ref, wdn_ref, wsw_ref, wo_ref, g_ref, out_ref):
    tm = x_ref.shape[0]
    subs = [slice(r, r + MERGE_SUB) for r in range(0, tm, MERGE_SUB)]

    def up(rows):
        return (jnp.dot(odn_ref[rows, :], wdn_ref[...], preferred_element_type=F32),
                jnp.dot(osw_ref[rows, :], wsw_ref[...], preferred_element_type=F32))

    def finish(rows, y_a, y_b):
        ga = gate_ref[rows, :D_MODEL].astype(F32)
        gb = gate_ref[rows, D_MODEL:].astype(F32)
        mix = _sigmoid(ga) * y_a + _sigmoid(gb) * y_b
        y = jnp.dot(mix.astype(BF16), wo_ref[...], preferred_element_type=F32)
        out_ref[rows, :] = x_ref[rows, :] + _rms(y, g_ref[...])

    pending = up(subs[0])
    for i, rows in enumerate(subs):
        nxt = up(subs[i + 1]) if i + 1 < len(subs) else None
        finish(rows, *pending)
        pending = nxt


def _merge(x, o_dn, o_sw, gates, w_dn, w_sw, w_o, layer, g, tm=1024):
    t = x.shape[0]
    row = lambda i: (i, 0)
    const = lambda i: (0, 0)
    wspec = pl.BlockSpec((None, D_MODEL, D_MODEL), lambda i: (layer, 0, 0))
    return pl.pallas_call(
        _merge_kernel,
        grid=(t // tm,),
        in_specs=[pl.BlockSpec((tm, D_MODEL), row), pl.BlockSpec((tm, DN_W), row),
                  pl.BlockSpec((tm, SW_Q_W), row), pl.BlockSpec((tm, GATE_COLS), row),
                  wspec, wspec, wspec, pl.BlockSpec((1, D_MODEL), const)],
        out_specs=pl.BlockSpec((tm, D_MODEL), row),
        out_shape=jax.ShapeDtypeStruct((t, D_MODEL), F32),
        compiler_params=pltpu.CompilerParams(dimension_semantics=("arbitrary",),
                                             vmem_limit_bytes=VMEM_LIMIT),
        name="merge",
    )(x, o_dn, o_sw, gates, w_dn, w_sw, w_o, g)


def _mlp_kernel(x_ref, gpre_ref, w1_ref, w2_ref, gpost_ref, out_ref):
    tm = x_ref.shape[0]
    subs = [slice(r, r + MLP_SUB) for r in range(0, tm, MLP_SUB)]

    def hidden(rows):
        return _rms(x_ref[rows, :], gpre_ref[...]).astype(BF16)

    def finish(rows, acc):
        out_ref[rows, :] = x_ref[rows, :] + _rms(acc, gpost_ref[...])

    h = hidden(subs[0])
    done = None
    for i, rows in enumerate(subs):
        acc = None
        h_next = None
        for f in range(0, D_FF, MLP_FF_CHUNK):
            a = jnp.maximum(jnp.dot(h, w1_ref[:, f:f + MLP_FF_CHUNK],
                                    preferred_element_type=F32), 0.0)
            t = jnp.dot((a * a).astype(BF16), w2_ref[f:f + MLP_FF_CHUNK, :],
                        preferred_element_type=F32)
            acc = t if acc is None else acc + t
            if f == 0:
                if i + 1 < len(subs):
                    h_next = hidden(subs[i + 1])
                if done is not None:
                    finish(*done)
        done = (rows, acc)
        h = h_next
    finish(*done)


def _mlp(x, g_pre, w1, w2, layer, g_post, tm=1024):
    t = x.shape[0]
    row = lambda i: (i, 0)
    const = lambda i: (0, 0)
    return pl.pallas_call(
        _mlp_kernel,
        grid=(t // tm,),
        in_specs=[pl.BlockSpec((tm, D_MODEL), row), pl.BlockSpec((1, D_MODEL), const),
                  pl.BlockSpec((None, D_MODEL, D_FF), lambda i: (layer, 0, 0),
                               pipeline_mode=pl.Buffered(1)),
                  pl.BlockSpec((None, D_FF, D_MODEL), lambda i: (layer, 0, 0),
                               pipeline_mode=pl.Buffered(1)),
                  pl.BlockSpec((1, D_MODEL), const)],
        out_specs=pl.BlockSpec((tm, D_MODEL), row),
        out_shape=jax.ShapeDtypeStruct((t, D_MODEL), F32),
        compiler_params=pltpu.CompilerParams(dimension_semantics=("arbitrary",),
                                             vmem_limit_bytes=VMEM_LIMIT),
        name="mlp",
    )(x, g_pre, w1, w2, g_post)


def _pack_w_in(w_in):
    o = 0
    dn = w_in[..., o:o + DN_COLS]
    o += DN_COLS
    ba = w_in[..., o:o + 2 * DN_HEADS]
    o += 2 * DN_HEADS
    sw = w_in[..., o:o + SW_COLS]
    o += SW_COLS
    gates = w_in[..., o:o + GATE_COLS]
    ba = jnp.pad(ba, [(0, 0)] * (ba.ndim - 1) + [(0, BA_COLS - 2 * DN_HEADS)])
    return jnp.concatenate([dn, sw, gates, ba], axis=-1).astype(BF16)


def _lane_row(vec, offset):
    return jnp.zeros((1, LANES), F32).at[0, offset:offset + vec.shape[0]].set(vec.astype(F32))


def kernel(x, positions, pre_mix_g, w_in, dn_conv_w, dn_a_log, dn_dt_bias, dn_norm_g, sw_sinks,
           w_up_dn, w_up_sw, w_o, post_mix_g, pre_mlp_g, w_ff1, w_ff2, post_mlp_g):
    batch, seq, _ = x.shape
    depth = w_in.shape[0]
    t = batch * seq
    xf = x.reshape(t, D_MODEL)
    cos_t, sin1_t, sin2_t = _rope_tables(positions)
    w_in_p = _pack_w_in(w_in)
    w_dn, w_sw, w_out = w_up_dn.astype(BF16), w_up_sw.astype(BF16), w_o.astype(BF16)
    w1, w2 = w_ff1.astype(BF16), w_ff2.astype(BF16)
    for l in range(depth):
        qkv, z, sw, gates, ba = _inproj(xf, pre_mix_g[l].reshape(1, D_MODEL), w_in_p, l,
                                        dn_conv_w[l], seq)
        o_dn = _deltanet(qkv, z, ba, _lane_row(dn_a_log[l], DN_HEADS),
                         _lane_row(dn_dt_bias[l], DN_HEADS), dn_norm_g[l].reshape(1, DN_DV),
                         batch, seq)
        o_sw = _swa(sw, sw_sinks[l].astype(F32), cos_t, sin1_t, sin2_t, batch, seq)
        xf = _merge(xf, o_dn, o_sw, gates, w_dn, w_sw, w_out, l,
                    post_mix_g[l].reshape(1, D_MODEL))
        xf = _mlp(xf, pre_mlp_g[l].reshape(1, D_MODEL), w1, w2, l,
                  post_mlp_g[l].reshape(1, D_MODEL))
    return xf.reshape(batch, seq, D_MODEL)
```

```python
import functools

import numpy as np
import jax
import jax.numpy as jnp
from jax import lax
from jax.experimental import pallas as pl
from jax.experimental.pallas import tpu as pltpu

F32 = jnp.float32
BF16 = jnp.bfloat16

D_MODEL = 1024
DN_HEADS = 8
DN_DK = 128
DN_DV = 128
DN_CONV = 4
SW_Q_HEADS = 16
SW_KV_HEADS = 2
SW_HEAD_DIM = 64
SW_BLOCK = 128
ROPE_THETA = 500000.0
ROT_DIM = SW_HEAD_DIM // 4
D_FF = 4 * D_MODEL
EPS = 1e-6

DN_W = DN_HEADS * DN_DK
SW_Q_W = SW_Q_HEADS * SW_HEAD_DIM
SW_KV_W = SW_KV_HEADS * SW_HEAD_DIM
LANES = 128

DN_COLS = 4 * DN_W
SW_COLS = SW_Q_W + 2 * SW_KV_W
GATE_COLS = 2 * D_MODEL
BA_COLS = LANES
PACKED_COLS = DN_COLS + SW_COLS + GATE_COLS + BA_COLS

DN_CHUNK = 128
MLP_SUB = 256
MLP_FF_CHUNK = 1024
MERGE_SUB = 256
INPROJ_CHUNK = 256
VMEM_LIMIT = 56 * 1024 * 1024


def _bdot(a, b):
    return jnp.dot(a.astype(BF16), b.astype(BF16), preferred_element_type=F32)


def _bdot_nt(a, b):
    return lax.dot_general(a.astype(BF16), b.astype(BF16), (((1,), (1,)), ((), ())),
                           preferred_element_type=F32)


def _bdot_tn(a, b):
    return lax.dot_general(a.astype(BF16), b.astype(BF16), (((0,), (0,)), ((), ())),
                           preferred_element_type=F32)


def _rms(x, g):
    return x * lax.rsqrt(jnp.mean(x * x, axis=-1, keepdims=True) + EPS) * g


def _sigmoid(x):
    return 1.0 / (1.0 + jnp.exp(-x))


def _inproj_kernel(x_ref, g_ref, w_ref, cw_ref, qkv_ref, z_ref, sw_ref, gate_ref, ba_ref,
                   tail_ref, pbuf_ref, ybuf_ref, *, tiles_per_seq):
    tm = x_ref.shape[0]

    @pl.when(pl.program_id(0) % tiles_per_seq == 0)
    def _():
        tail_ref[...] = jnp.zeros_like(tail_ref)

    h = _rms(x_ref[...], g_ref[...]).astype(BF16)

    def proj(off, width):
        return jnp.dot(h, w_ref[:, off:off + width], preferred_element_type=F32)

    def conv_matmul(c):
        p = proj(c, INPROJ_CHUNK)
        for hh in range(0, INPROJ_CHUNK, DN_DK):
            pbuf_ref[(c + hh) // DN_DK] = p[:, hh:hh + DN_DK]

    def conv_epilogue(c):
        rows = tm // 8
        for hh in range(0, INPROJ_CHUNK, DN_DK):
            g = (c + hh) // DN_DK
            cols = slice(c + hh, c + hh + DN_DK)
            x = [pbuf_ref[g, pl.ds(j, rows, stride=8), :] for j in range(8)]
            prev = {}
            for j in range(8 - (DN_CONV - 1), 8):
                prev[j - 8] = jnp.concatenate([tail_ref[j:j + 1, cols], x[j][:rows - 1, :]],
                                              axis=0)
            tail_ref[:, cols] = pbuf_ref[g, tm - 8:tm, :]
            tok = lambda j: x[j] if j >= 0 else prev[j]
            for j in range(8):
                y = x[j] * cw_ref[DN_CONV - 1:DN_CONV, cols]
                for i in range(DN_CONV - 1):
                    y = y + tok(j - (DN_CONV - 1 - i)) * cw_ref[i:i + 1, cols]
                y = y * _sigmoid(y)
                if c + hh < 2 * DN_W:
                    scale = DN_DK ** -0.5 if c + hh < DN_W else 1.0
                    y = y * (lax.rsqrt(jnp.sum(y * y, axis=-1, keepdims=True) + EPS) * scale)
                ybuf_ref[g, pl.ds(j, rows, stride=8), :] = y
            qkv_ref[:, cols] = ybuf_ref[g].astype(qkv_ref.dtype)

    def silu_chunk(c):
        zc = proj(3 * DN_W + c, INPROJ_CHUNK)
        z_ref[:, c:c + INPROJ_CHUNK] = (zc * _sigmoid(zc)).astype(z_ref.dtype)

    def plain_chunk(ref, off, c, w):
        ref[:, c:c + w] = proj(off + c, w).astype(ref.dtype)

    conv_cols = list(range(0, 3 * DN_W, INPROJ_CHUNK))
    light = [functools.partial(silu_chunk, c) for c in range(0, DN_W, INPROJ_CHUNK)]
    off = DN_COLS
    for ref, width in ((sw_ref, SW_COLS), (gate_ref, GATE_COLS), (ba_ref, BA_COLS)):
        for c in range(0, width, INPROJ_CHUNK):
            light.append(functools.partial(plain_chunk, ref, off, c, min(INPROJ_CHUNK, width - c)))
        off += width
    per_conv = -(-len(light) // len(conv_cols))
    conv_matmul(conv_cols[0])
    for i, c in enumerate(conv_cols):
        for fn in light[i * per_conv:(i + 1) * per_conv]:
            fn()
        if i + 1 < len(conv_cols):
            conv_matmul(conv_cols[i + 1])
        conv_epilogue(c)


def _inproj(x, g, w, layer, conv_w, seq, tm=512):
    t = x.shape[0]
    row = lambda i: (i, 0)
    const = lambda i: (0, 0)
    return pl.pallas_call(
        functools.partial(_inproj_kernel, tiles_per_seq=seq // tm),
        grid=(t // tm,),
        in_specs=[pl.BlockSpec((tm, D_MODEL), row),
                  pl.BlockSpec((1, D_MODEL), const),
                  pl.BlockSpec((None, D_MODEL, PACKED_COLS), lambda i: (layer, 0, 0),
                               pipeline_mode=pl.Buffered(1)),
                  pl.BlockSpec((DN_CONV, 3 * DN_W), const)],
        out_specs=[pl.BlockSpec((tm, 3 * DN_W), row), pl.BlockSpec((tm, DN_W), row),
                   pl.BlockSpec((tm, SW_COLS), row),
                   pl.BlockSpec((tm, GATE_COLS), row), pl.BlockSpec((tm, BA_COLS), row)],
        out_shape=[jax.ShapeDtypeStruct((t, 3 * DN_W), BF16),
                   jax.ShapeDtypeStruct((t, DN_W), BF16),
                   jax.ShapeDtypeStruct((t, SW_COLS), BF16),
                   jax.ShapeDtypeStruct((t, GATE_COLS), BF16),
                   jax.ShapeDtypeStruct((t, BA_COLS), F32)],
        scratch_shapes=[pltpu.VMEM((8, 3 * DN_W), F32), pltpu.VMEM((3 * DN_HEADS, tm, LANES), F32),
                        pltpu.VMEM((3 * DN_HEADS, tm, LANES), F32)],
        compiler_params=pltpu.CompilerParams(dimension_semantics=("arbitrary",),
                                             vmem_limit_bytes=VMEM_LIMIT),
        name="inproj",
    )(x, g, w, conv_w)


def _deltanet_kernel(qkv_ref, z_ref, ba_ref, alog_ref, dtb_ref, ng_ref, o_ref,
                     state_ref, u_scr, wq_scr, ai_scr, kd_scr, el_scr):
    C = DN_CHUNK
    B = qkv_ref.shape[0]
    s = pl.program_id(0)

    @pl.when(s == 0)
    def _():
        u_scr[...] = jnp.zeros_like(u_scr)
        wq_scr[...] = jnp.zeros_like(wq_scr)
        ai_scr[...] = jnp.zeros_like(ai_scr)
        kd_scr[...] = jnp.zeros_like(kd_scr)
        el_scr[...] = jnp.zeros_like(el_scr)
        state_ref[...] = jnp.zeros_like(state_ref)

    lane = lax.broadcasted_iota(jnp.int32, (C, LANES), 1)
    row = lax.broadcasted_iota(jnp.int32, (C, LANES), 0)
    causal = row >= lane
    strict = row > lane
    eye = jnp.where(row == lane, 1.0, 0.0)
    sib = row ^ lane
    BH = [(b, h) for b in range(B) for h in range(DN_HEADS)]
    N = range(len(BH))

    def lane_bcast(x, j):
        return jnp.broadcast_to(x[:, j:j + 1], (x.shape[0], LANES))

    state = [state_ref[i] for i in N]
    el = [lane_bcast(el_scr[b, 0:1, :], DN_HEADS + h) for b, h in BH]
    ws = [_bdot(wq_scr[i], state[i]) for i in N]

    tril = jnp.where(causal, 1.0, 0.0).astype(BF16)
    gc_all, gc_all_t, beta_all, eg_all, ed_all, el_new = [], [], [], [], [], []
    for b in range(B):
        ba = ba_ref[b]
        xg = ba + dtb_ref[...]
        softplus = jnp.maximum(xg, 0.0) + jnp.log(1.0 + jnp.exp(-jnp.abs(xg)))
        g_all = -jnp.exp(alog_ref[...]) * softplus
        g_hi = g_all.astype(BF16)
        r1 = g_all - g_hi.astype(F32)
        g_mid = r1.astype(BF16)
        g_lo = (r1 - g_mid.astype(F32)).astype(BF16)
        gc = (jnp.dot(tril, g_hi, preferred_element_type=F32)
              + jnp.dot(tril, g_mid, preferred_element_type=F32)
              + jnp.dot(tril, g_lo, preferred_element_type=F32))
        g_last = gc[C - 1:C, :]
        gc_all.append(gc)
        gc_all_t.append(gc.T)
        beta_all.append(_sigmoid(ba).astype(BF16))
        eg_all.append(jnp.exp(gc).astype(BF16))
        ed_all.append(jnp.exp(g_last - gc).astype(BF16))
        el_new.append(jnp.exp(g_last))

    q = [qkv_ref[b, :, h * DN_DK:(h + 1) * DN_DK] for b, h in BH]
    k = [qkv_ref[b, :, DN_W + h * DN_DK:DN_W + (h + 1) * DN_DK] for b, h in BH]
    v = [qkv_ref[b, :, 2 * DN_W + h * DN_DV:2 * DN_W + (h + 1) * DN_DV] for b, h in BH]
    beta = [lane_bcast(beta_all[b], h) for b, h in BH]
    eg = [lane_bcast(eg_all[b], DN_HEADS + h) for b, h in BH]
    ed = [lane_bcast(ed_all[b], DN_HEADS + h) for b, h in BH]
    kb = [k[i] * beta[i] for i in N]
    kk = [_bdot_nt(kb[i], k[i]) for i in N]
    qk = [_bdot_nt(q[i], k[i]) for i in N]

    v_new = [u_scr[i] - ws[i][:C] for i in N]
    av = [_bdot(ai_scr[i], v_new[i]) for i in N]
    kv = [_bdot_tn(kd_scr[i], v_new[i]) for i in N]

    gc = [lane_bcast(gc_all[b], DN_HEADS + h) for b, h in BH]
    gc_t = [jnp.broadcast_to(gc_all_t[b][DN_HEADS + h:DN_HEADS + h + 1, :], (C, LANES))
            for b, h in BH]
    decay = [jnp.where(causal, jnp.exp(jnp.where(causal, gc[i] - gc_t[i], 0.0)), 0.0) for i in N]
    a_mat = [jnp.where(strict, kk[i] * decay[i], 0.0) for i in N]
    a_intra = [jnp.where(causal, qk[i] * decay[i], 0.0).astype(BF16) for i in N]
    t_mat = [eye - jnp.where(sib < 2, a_mat[i], 0.0) for i in N]

    def level(size, t_mat):
        off = (sib >= size) & (sib < 2 * size)
        if size < 8:
            x_mat = [_bdot(jnp.where(off, a_mat[i], 0.0), t_mat[i]) for i in N]
            return [t_mat[i] - _bdot(t_mat[i], x_mat[i]) for i in N]
        odd = [r for r in range(0, C, size) if (r // size) % 2 == 1]
        rows = lambda m: jnp.concatenate([m[r:r + size] for r in odd], axis=0)
        off_rows = rows(jnp.where(off, 1.0, 0.0)) > 0.5
        zero = jnp.zeros((size, C), F32)
        x_half = [_bdot(jnp.where(off_rows, rows(a_mat[i]), 0.0), t_mat[i]) for i in N]
        t_rows = [rows(t_mat[i]) for i in N]
        x_full = [jnp.concatenate(
            [x_half[i][(r // (2 * size)) * size:(r // (2 * size) + 1) * size]
             if r in odd else zero for r in range(0, C, size)], axis=0) for i in N]
        new_rows = [t_rows[i] - _bdot(t_rows[i], x_full[i]) for i in N]
        return [jnp.concatenate(
            [new_rows[i][(r // (2 * size)) * size:(r // (2 * size) + 1) * size]
             if r in odd else t_mat[i][r:r + size] for r in range(0, C, size)], axis=0)
            for i in N]

    t_mat = level(2, t_mat)

    for i, (b, h) in enumerate(BH):
        state_ref[i] = state[i] * el[i] + kv[i]
        o_i = ws[i][C:] + av[i]
        o_ref[b, :, h * DN_DV:(h + 1) * DN_DV] = (
            _rms(o_i, ng_ref[...]) * z_ref[b, :, h * DN_DV:(h + 1) * DN_DV].astype(F32)
        ).astype(o_ref.dtype)

    size = 4
    while size < C:
        t_mat = level(size, t_mat)
        size *= 2
    uw = [_bdot(t_mat[i], jnp.concatenate([v[i] * beta[i], kb[i] * eg[i]], axis=1))
          for i in N]
    for i, (b, h) in enumerate(BH):
        u_scr[i] = uw[i][:, :DN_DV]
        wq_scr[i, 0:C, :] = uw[i][:, DN_DV:].astype(BF16)
        wq_scr[i, C:2 * C, :] = q[i] * eg[i]
        ai_scr[i] = a_intra[i]
        kd_scr[i] = k[i] * ed[i]
    for b in range(B):
        el_scr[b] = jnp.broadcast_to(el_new[b], el_scr.shape[1:])


def _deltanet(qkv, z, ba, alog_row, dtb_row, norm_g, batch, seq):
    C = DN_CHUNK
    nc = seq // C
    n_bh = batch * DN_HEADS
    cur = lambda s: (0, jnp.minimum(s, nc - 1), 0)
    prev = lambda s: (0, jnp.maximum(s - 1, 0), 0)
    const = lambda s: (0, 0)
    out = pl.pallas_call(
        _deltanet_kernel,
        grid=(nc + 1,),
        in_specs=[pl.BlockSpec((batch, C, 3 * DN_W), cur),
                  pl.BlockSpec((batch, C, DN_W), prev),
                  pl.BlockSpec((batch, C, LANES), cur),
                  pl.BlockSpec((1, LANES), const), pl.BlockSpec((1, LANES), const),
                  pl.BlockSpec((1, LANES), const)],
        out_specs=pl.BlockSpec((batch, C, DN_W), prev),
        out_shape=jax.ShapeDtypeStruct((batch, seq, DN_W), BF16),
        scratch_shapes=[pltpu.VMEM((n_bh, DN_DK, DN_DV), F32),
                        pltpu.VMEM((n_bh, C, DN_DV), F32),
                        pltpu.VMEM((n_bh, 2 * C, DN_DK), BF16),
                        pltpu.VMEM((n_bh, C, C), BF16),
                        pltpu.VMEM((n_bh, C, DN_DK), BF16),
                        pltpu.VMEM((batch, 8, LANES), F32)],
        compiler_params=pltpu.CompilerParams(dimension_semantics=("arbitrary",),
                                             vmem_limit_bytes=VMEM_LIMIT),
        name="deltanet",
    )(qkv.reshape(batch, seq, 3 * DN_W), z.reshape(batch, seq, DN_W),
      ba.reshape(batch, seq, LANES), alog_row, dtb_row, norm_g)
    return out.reshape(batch * seq, DN_W)


def _rope_table_kernel(pos_ref, freq_ref, place_ref, c_ref, s1_ref, s2_ref):
    ang = freq_ref[...] * pos_ref[0].astype(F32)
    for out_ref, val, k in ((c_ref, jnp.cos(ang) - 1.0, 0), (s1_ref, -jnp.sin(ang), 1),
                            (s2_ref, jnp.sin(ang), 2)):
        hi = val.astype(BF16)
        r1 = val - hi.astype(F32)
        mid = r1.astype(BF16)
        lo = (r1 - mid.astype(F32)).astype(BF16)
        place = place_ref[k]
        acc = sum(lax.dot_general(t, place, (((0,), (0,)), ((), ())),
                                  preferred_element_type=F32) for t in (hi, mid, lo))
        out_ref[...] = acc + 1.0 if k == 0 else acc


def _rope_tables(positions, tm=2048):
    t = positions.size
    half = ROT_DIM // 2
    lane = np.arange(LANES)
    within = lane % SW_HEAD_DIM
    inv_freq = ROPE_THETA ** (-np.arange(half, dtype=np.float32) * (2.0 / ROT_DIM))
    onehot = (within[None, :] % half == np.arange(half)[:, None])
    place = np.stack([onehot & (within < ROT_DIM)[None, :],
                      onehot & (within < half)[None, :],
                      onehot & ((within >= half) & (within < ROT_DIM))[None, :]]).astype(np.float32)
    out = jax.ShapeDtypeStruct((t, LANES), F32)
    return pl.pallas_call(
        _rope_table_kernel,
        grid=(t // tm,),
        in_specs=[pl.BlockSpec((None, 1, tm), lambda i: (i, 0, 0)),
                  pl.BlockSpec((half, 1), lambda i: (0, 0)),
                  pl.BlockSpec((3, half, LANES), lambda i: (0, 0, 0))],
        out_specs=[pl.BlockSpec((tm, LANES), lambda i: (i, 0))] * 3,
        out_shape=[out, out, out],
        name="rope_tables",
    )(positions.reshape(t // tm, 1, tm), jnp.asarray(inv_freq.reshape(half, 1)),
      jnp.asarray(place, dtype=BF16))


def _swa_kernel(sink_ref, q_ref, kv_ref, kvn_ref, c_ref, s1_ref, s2_ref, cn_ref, s1n_ref, s2n_ref,
                o_ref, k_scr, v_scr, *, blocks_per_seq):
    Q = SW_BLOCK
    step = pl.program_id(0)
    half = SW_HEAD_DIM
    lane = lax.broadcasted_iota(jnp.int32, (Q, LANES), 1)
    low = lane < half
    log2e = 1.4426950408889634

    def rope(x, cos, sin1, sin2):
        return x * cos + pltpu.roll(x, LANES - ROT_DIM // 2, 1) * sin1 \
            + pltpu.roll(x, ROT_DIM // 2, 1) * sin2

    def prepare(kv_tile, cos, sin1, sin2, dst):
        kv = kv_tile.astype(F32)
        kr = rope(kv[:, :LANES], cos, sin1, sin2)
        kr_sw = pltpu.roll(kr, half, 1)
        vt = kv[:, LANES:]
        vt_sw = pltpu.roll(vt, half, 1)
        ones_lo = jnp.where(lane == half, 1.0, 0.0)
        ones_hi = jnp.where(lane == 0, 1.0, 0.0)
        k_scr[0, dst, :] = jnp.where(low, kr, kr_sw).astype(BF16)
        k_scr[1, dst, :] = jnp.where(low, kr_sw, kr).astype(BF16)
        v_scr[0, dst, :] = jnp.where(low, vt, ones_lo).astype(BF16)
        v_scr[1, dst, :] = jnp.where(low, ones_hi, vt_sw).astype(BF16)
        v_scr[2, dst, :] = jnp.where(low, vt_sw, ones_lo).astype(BF16)
        v_scr[3, dst, :] = jnp.where(low, ones_hi, vt).astype(BF16)

    @pl.when(step == 0)
    def _():
        k_scr[...] = jnp.zeros_like(k_scr)
        v_scr[...] = jnp.zeros_like(v_scr)
        prepare(kv_ref[...], c_ref[...], s1_ref[...], s2_ref[...], slice(Q, 2 * Q))

    @pl.when(step > 0)
    def _():
        for j in range(SW_KV_HEADS):
            k_scr[j, 0:2 * Q, :] = k_scr[j, Q:3 * Q, :]
        for j in range(2 * SW_KV_HEADS):
            v_scr[j, 0:2 * Q, :] = v_scr[j, Q:3 * Q, :]

    first_of_seq = (step % blocks_per_seq) == 0
    qi = lax.broadcasted_iota(jnp.int32, (Q, 2 * Q), 0)
    ki = lax.broadcasted_iota(jnp.int32, (Q, 2 * Q), 1)
    valid = (ki > qi) & (ki <= qi + Q) & ((ki >= Q) | jnp.logical_not(first_of_seq))
    cos = c_ref[...]
    sin1 = s1_ref[...]
    sin2 = s2_ref[...]

    pairs_per_kv = SW_Q_HEADS // SW_KV_HEADS // 2
    heads = []
    for pair in range(SW_Q_HEADS // 2):
        heads += [(pair, pair // pairs_per_kv, False), (pair, pair // pairs_per_kv, True)]
    qp = [rope(q_ref[:, pair * LANES:(pair + 1) * LANES].astype(F32), cos, sin1, sin2)
          * (SW_HEAD_DIM ** -0.5 * log2e) for pair in range(SW_Q_HEADS // 2)]
    s = [_bdot_nt(jnp.where(low, 0.0, qp[pair]) if hi else jnp.where(low, qp[pair], 0.0),
                  k_scr[j, 0:2 * Q, :]) for pair, j, hi in heads]
    sink = [sink_ref[2 * pair + int(hi)] * log2e for pair, j, hi in heads]
    s = [jnp.where(valid, t, -jnp.inf) for t in s]
    m = [jnp.maximum(jnp.max(s[i], axis=-1, keepdims=True), sink[i]) for i in range(len(heads))]
    p = [jnp.exp2(s[i] - m[i]) for i in range(len(heads))]
    pv = [_bdot(p[i], v_scr[2 * j + int(hi), 0:2 * Q, :]) for i, (pair, j, hi) in enumerate(heads)]
    prepare(kvn_ref[...], cn_ref[...], s1n_ref[...], s2n_ref[...], slice(2 * Q, 3 * Q))
    out = []
    for i, (pair, j, hi) in enumerate(heads):
        sum_lane = 0 if hi else half
        denom = pv[i][:, sum_lane:sum_lane + 1] + jnp.exp2(sink[i] - m[i])
        out.append(pv[i] * (1.0 / denom))
    for i in range(0, len(heads), 2):
        pair = heads[i][0]
        o_ref[:, pair * LANES:(pair + 1) * LANES] = jnp.where(
            low, out[i], out[i + 1]).astype(o_ref.dtype)


def _swa(sw, sinks, cos_t, sin1_t, sin2_t, batch, seq):
    t = sw.shape[0]
    Q = SW_BLOCK
    total = t // Q
    blk = lambda i, s: (i, 0)
    nxt = lambda i, s: (jnp.minimum(i + 1, total - 1), 0)
    kv_col = SW_Q_W // (2 * SW_KV_W)
    return pl.pallas_call(
        functools.partial(_swa_kernel, blocks_per_seq=seq // Q),
        grid_spec=pltpu.PrefetchScalarGridSpec(
            num_scalar_prefetch=1,
            grid=(total,),
            in_specs=[pl.BlockSpec((Q, SW_Q_W), blk),
                      pl.BlockSpec((Q, 2 * SW_KV_W), lambda i, s: (i, kv_col)),
                      pl.BlockSpec((Q, 2 * SW_KV_W),
                                   lambda i, s: (jnp.minimum(i + 1, total - 1), kv_col)),
                      pl.BlockSpec((Q, LANES), blk), pl.BlockSpec((Q, LANES), blk),
                      pl.BlockSpec((Q, LANES), blk),
                      pl.BlockSpec((Q, LANES), nxt), pl.BlockSpec((Q, LANES), nxt),
                      pl.BlockSpec((Q, LANES), nxt)],
            out_specs=pl.BlockSpec((Q, SW_Q_W), blk),
            scratch_shapes=[pltpu.VMEM((SW_KV_HEADS, 3 * Q, LANES), BF16),
                            pltpu.VMEM((2 * SW_KV_HEADS, 3 * Q, LANES), BF16)]),
        out_shape=jax.ShapeDtypeStruct((t, SW_Q_W), BF16),
        compiler_params=pltpu.CompilerParams(dimension_semantics=("arbitrary",),
                                             vmem_limit_bytes=VMEM_LIMIT),
        name="swa",
    )(sinks, sw, sw, sw, cos_t, sin1_t, sin2_t, cos_t, sin1_t, sin2_t)


def _merge_kernel(x_ref, odn_ref, osw_ref, gate_ref, wdn_ref, wsw_ref, wo_ref, g_ref, out_ref):
    tm = x_ref.shape[0]
    subs = [slice(r, r + MERGE_SUB) for r in range(0, tm, MERGE_SUB)]

    def up(rows):
        return (jnp.dot(odn_ref[rows, :], wdn_ref[...], preferred_element_type=F32),
                jnp.dot(osw_ref[rows, :], wsw_ref[...], preferred_element_type=F32))

    def finish(rows, y_a, y_b):
        ga = gate_ref[rows, :D_MODEL].astype(F32)
        gb = gate_ref[rows, D_MODEL:].astype(F32)
        mix = _sigmoid(ga) * y_a + _sigmoid(gb) * y_b
        y = jnp.dot(mix.astype(BF16), wo_ref[...], preferred_element_type=F32)
        out_ref[rows, :] = x_ref[rows, :] + _rms(y, g_ref[...])

    pending = up(subs[0])
    for i, rows in enumerate(subs):
        nxt = up(subs[i + 1]) if i + 1 < len(subs) else None
        finish(rows, *pending)
        pending = nxt


def _merge(x, o_dn, o_sw, gates, w_dn, w_sw, w_o, layer, g, tm=1024):
    t = x.shape[0]
    row = lambda i: (i, 0)
    const = lambda i: (0, 0)
    wspec = pl.BlockSpec((None, D_MODEL, D_MODEL), lambda i: (layer, 0, 0))
    return pl.pallas_call(
        _merge_kernel,
        grid=(t // tm,),
        in_specs=[pl.BlockSpec((tm, D_MODEL), row), pl.BlockSpec((tm, DN_W), row),
                  pl.BlockSpec((tm, SW_Q_W), row), pl.BlockSpec((tm, GATE_COLS), row),
                  wspec, wspec, wspec, pl.BlockSpec((1, D_MODEL), const)],
        out_specs=pl.BlockSpec((tm, D_MODEL), row),
        out_shape=jax.ShapeDtypeStruct((t, D_MODEL), F32),
        compiler_params=pltpu.CompilerParams(dimension_semantics=("arbitrary",),
                                             vmem_limit_bytes=VMEM_LIMIT),
        name="merge",
    )(x, o_dn, o_sw, gates, w_dn, w_sw, w_o, g)


def _mlp_kernel(x_ref, gpre_ref, w1_ref, w2_ref, gpost_ref, out_ref):
    tm = x_ref.shape[0]
    subs = [slice(r, r + MLP_SUB) for r in range(0, tm, MLP_SUB)]

    def hidden(rows):
        return _rms(x_ref[rows, :], gpre_ref[...]).astype(BF16)

    def finish(rows, acc):
        out_ref[rows, :] = x_ref[rows, :] + _rms(acc, gpost_ref[...])

    h = hidden(subs[0])
    done = None
    for i, rows in enumerate(subs):
        acc = None
        h_next = None
        for f in range(0, D_FF, MLP_FF_CHUNK):
            a = jnp.maximum(jnp.dot(h, w1_ref[:, f:f + MLP_FF_CHUNK],
                                    preferred_element_type=F32), 0.0)
            t = jnp.dot((a * a).astype(BF16), w2_ref[f:f + MLP_FF_CHUNK, :],
                        preferred_element_type=F32)
            acc = t if acc is None else acc + t
            if f == 0:
                if i + 1 < len(subs):
                    h_next = hidden(subs[i + 1])
                if done is not None:
                    finish(*done)
        done = (rows, acc)
        h = h_next
    finish(*done)


def _mlp(x, g_pre, w1, w2, layer, g_post, tm=1024):
    t = x.shape[0]
    row = lambda i: (i, 0)
    const = lambda i: (0, 0)
    return pl.pallas_call(
        _mlp_kernel,
        grid=(t // tm,),
        in_specs=[pl.BlockSpec((tm, D_MODEL), row), pl.BlockSpec((1, D_MODEL), const),
                  pl.BlockSpec((None, D_MODEL, D_FF), lambda i: (layer, 0, 0),
                               pipeline_mode=pl.Buffered(1)),
                  pl.BlockSpec((None, D_FF, D_MODEL), lambda i: (layer, 0, 0),
                               pipeline_mode=pl.Buffered(1)),
                  pl.BlockSpec((1, D_MODEL), const)],
        out_specs=pl.BlockSpec((tm, D_MODEL), row),
        out_shape=jax.ShapeDtypeStruct((t, D_MODEL), F32),
        compiler_params=pltpu.CompilerParams(dimension_semantics=("arbitrary",),
                                             vmem_limit_bytes=VMEM_LIMIT),
        name="mlp",
    )(x, g_pre, w1, w2, g_post)


def _pack_w_in_kernel(w_ref, out_ref):
    n_ba = 2 * DN_HEADS
    out_ref[:, :DN_COLS] = w_ref[:, :DN_COLS].astype(BF16)
    rest = w_ref[:, DN_COLS:]
    out_ref[:, DN_COLS:DN_COLS + SW_COLS + GATE_COLS] = rest[:, n_ba:].astype(BF16)
    tail = jnp.concatenate([rest[:, :n_ba], jnp.zeros((rest.shape[0], BA_COLS - n_ba), F32)],
                           axis=1)
    out_ref[:, DN_COLS + SW_COLS + GATE_COLS:] = tail.astype(BF16)


def _pack_w_in(w_in, tr=256):
    depth, rows, d_in = w_in.shape
    return pl.pallas_call(
        _pack_w_in_kernel,
        grid=(depth, rows // tr),
        in_specs=[pl.BlockSpec((None, tr, d_in), lambda l, i: (l, i, 0))],
        out_specs=pl.BlockSpec((None, tr, PACKED_COLS), lambda l, i: (l, i, 0)),
        out_shape=jax.ShapeDtypeStruct((depth, rows, PACKED_COLS), BF16),
        compiler_params=pltpu.CompilerParams(vmem_limit_bytes=VMEM_LIMIT),
        name="pack_w_in",
    )(w_in)


def _lane_row(vec, offset):
    return jnp.zeros((1, LANES), F32).at[0, offset:offset + vec.shape[0]].set(vec.astype(F32))


def kernel(x, positions, pre_mix_g, w_in, dn_conv_w, dn_a_log, dn_dt_bias, dn_norm_g, sw_sinks,
           w_up_dn, w_up_sw, w_o, post_mix_g, pre_mlp_g, w_ff1, w_ff2, post_mlp_g):
    batch, seq, _ = x.shape
    depth = w_in.shape[0]
    t = batch * seq
    xf = x.reshape(t, D_MODEL)
    cos_t, sin1_t, sin2_t = _rope_tables(positions)
    w_in_p = _pack_w_in(w_in)
    w_dn, w_sw, w_out = w_up_dn.astype(BF16), w_up_sw.astype(BF16), w_o.astype(BF16)
    w1, w2 = w_ff1.astype(BF16), w_ff2.astype(BF16)
    for l in range(depth):
        qkv, z, sw, gates, ba = _inproj(xf, pre_mix_g[l].reshape(1, D_MODEL), w_in_p, l,
                                        dn_conv_w[l], seq)
        o_dn = _deltanet(qkv, z, ba, _lane_row(dn_a_log[l], DN_HEADS),
                         _lane_row(dn_dt_bias[l], DN_HEADS), dn_norm_g[l].reshape(1, DN_DV),
                         batch, seq)
        o_sw = _swa(sw, sw_sinks[l].astype(F32), cos_t, sin1_t, sin2_t, batch, seq)
        xf = _merge(xf, o_dn, o_sw, gates, w_dn, w_sw, w_out, l,
                    post_mix_g[l].reshape(1, D_MODEL))
        xf = _mlp(xf, pre_mlp_g[l].reshape(1, D_MODEL), w1, w2, l,
                  post_mlp_g[l].reshape(1, D_MODEL))
    return xf.reshape(batch, seq, D_MODEL)
```

```python
import functools

import numpy as np
import jax
import jax.numpy as jnp
from jax import lax
from jax.experimental import pallas as pl
from jax.experimental.pallas import tpu as pltpu

F32 = jnp.float32
BF16 = jnp.bfloat16

D_MODEL = 1024
DN_HEADS = 8
DN_DK = 128
DN_DV = 128
DN_CONV = 4
SW_Q_HEADS = 16
SW_KV_HEADS = 2
SW_HEAD_DIM = 64
SW_BLOCK = 128
ROPE_THETA = 500000.0
ROT_DIM = SW_HEAD_DIM // 4
D_FF = 4 * D_MODEL
EPS = 1e-6

DN_W = DN_HEADS * DN_DK
SW_Q_W = SW_Q_HEADS * SW_HEAD_DIM
SW_KV_W = SW_KV_HEADS * SW_HEAD_DIM
LANES = 128

DN_COLS = 4 * DN_W
SW_COLS = SW_Q_W + 2 * SW_KV_W
GATE_COLS = 2 * D_MODEL
BA_COLS = LANES
PACKED_COLS = DN_COLS + SW_COLS + GATE_COLS + BA_COLS

DN_CHUNK = 128
MLP_SUB = 256
MLP_FF_CHUNK = 1024
MERGE_SUB = 256
INPROJ_CHUNK = 256
VMEM_LIMIT = 56 * 1024 * 1024


def _bdot(a, b):
    return jnp.dot(a.astype(BF16), b.astype(BF16), preferred_element_type=F32)


def _bdot_nt(a, b):
    return lax.dot_general(a.astype(BF16), b.astype(BF16), (((1,), (1,)), ((), ())),
                           preferred_element_type=F32)


def _bdot_tn(a, b):
    return lax.dot_general(a.astype(BF16), b.astype(BF16), (((0,), (0,)), ((), ())),
                           preferred_element_type=F32)


def _rms(x, g):
    return x * lax.rsqrt(jnp.mean(x * x, axis=-1, keepdims=True) + EPS) * g


def _sigmoid(x):
    return 1.0 / (1.0 + jnp.exp(-x))


def _inproj_kernel(x_ref, g_ref, w_ref, cw_ref, qkv_ref, z_ref, sw_ref, gate_ref, ba_ref,
                   tail_ref, pbuf_ref, ybuf_ref, *, tiles_per_seq):
    tm = x_ref.shape[0]

    @pl.when(pl.program_id(0) % tiles_per_seq == 0)
    def _():
        tail_ref[...] = jnp.zeros_like(tail_ref)

    h = _rms(x_ref[...], g_ref[...]).astype(BF16)

    def proj(off, width):
        return jnp.dot(h, w_ref[:, off:off + width], preferred_element_type=F32)

    def conv_matmul(c):
        p = proj(c, INPROJ_CHUNK)
        for hh in range(0, INPROJ_CHUNK, DN_DK):
            pbuf_ref[(c + hh) // DN_DK] = p[:, hh:hh + DN_DK]

    def conv_epilogue(c):
        rows = tm // 8
        for hh in range(0, INPROJ_CHUNK, DN_DK):
            g = (c + hh) // DN_DK
            cols = slice(c + hh, c + hh + DN_DK)
            x = [pbuf_ref[g, pl.ds(j, rows, stride=8), :] for j in range(8)]
            prev = {}
            for j in range(8 - (DN_CONV - 1), 8):
                prev[j - 8] = jnp.concatenate([tail_ref[j:j + 1, cols], x[j][:rows - 1, :]],
                                              axis=0)
            tail_ref[:, cols] = pbuf_ref[g, tm - 8:tm, :]
            tok = lambda j: x[j] if j >= 0 else prev[j]
            for j in range(8):
                y = x[j] * cw_ref[DN_CONV - 1:DN_CONV, cols]
                for i in range(DN_CONV - 1):
                    y = y + tok(j - (DN_CONV - 1 - i)) * cw_ref[i:i + 1, cols]
                y = y * _sigmoid(y)
                if c + hh < 2 * DN_W:
                    scale = DN_DK ** -0.5 if c + hh < DN_W else 1.0
                    y = y * (lax.rsqrt(jnp.sum(y * y, axis=-1, keepdims=True) + EPS) * scale)
                ybuf_ref[g, pl.ds(j, rows, stride=8), :] = y
            qkv_ref[:, cols] = ybuf_ref[g].astype(qkv_ref.dtype)

    def silu_chunk(c):
        zc = proj(3 * DN_W + c, INPROJ_CHUNK)
        z_ref[:, c:c + INPROJ_CHUNK] = (zc * _sigmoid(zc)).astype(z_ref.dtype)

    def plain_chunk(ref, off, c, w):
        ref[:, c:c + w] = proj(off + c, w).astype(ref.dtype)

    conv_cols = list(range(0, 3 * DN_W, INPROJ_CHUNK))
    light = [functools.partial(silu_chunk, c) for c in range(0, DN_W, INPROJ_CHUNK)]
    off = DN_COLS
    for ref, width in ((sw_ref, SW_COLS), (gate_ref, GATE_COLS), (ba_ref, BA_COLS)):
        for c in range(0, width, INPROJ_CHUNK):
            light.append(functools.partial(plain_chunk, ref, off, c, min(INPROJ_CHUNK, width - c)))
        off += width
    per_conv = -(-len(light) // len(conv_cols))
    conv_matmul(conv_cols[0])
    for i, c in enumerate(conv_cols):
        for fn in light[i * per_conv:(i + 1) * per_conv]:
            fn()
        if i + 1 < len(conv_cols):
            conv_matmul(conv_cols[i + 1])
        conv_epilogue(c)


def _inproj(x, g, w, layer, conv_w, seq, tm=256):
    t = x.shape[0]
    row = lambda i: (i, 0)
    const = lambda i: (0, 0)
    return pl.pallas_call(
        functools.partial(_inproj_kernel, tiles_per_seq=seq // tm),
        grid=(t // tm,),
        in_specs=[pl.BlockSpec((tm, D_MODEL), row),
                  pl.BlockSpec((1, D_MODEL), const),
                  pl.BlockSpec((None, D_MODEL, PACKED_COLS), lambda i: (layer, 0, 0),
                               pipeline_mode=pl.Buffered(1)),
                  pl.BlockSpec((DN_CONV, 3 * DN_W), const)],
        out_specs=[pl.BlockSpec((tm, 3 * DN_W), row), pl.BlockSpec((tm, DN_W), row),
                   pl.BlockSpec((tm, SW_COLS), row),
                   pl.BlockSpec((tm, GATE_COLS), row), pl.BlockSpec((tm, BA_COLS), row)],
        out_shape=[jax.ShapeDtypeStruct((t, 3 * DN_W), BF16),
                   jax.ShapeDtypeStruct((t, DN_W), BF16),
                   jax.ShapeDtypeStruct((t, SW_COLS), BF16),
                   jax.ShapeDtypeStruct((t, GATE_COLS), BF16),
                   jax.ShapeDtypeStruct((t, BA_COLS), F32)],
        scratch_shapes=[pltpu.VMEM((8, 3 * DN_W), F32), pltpu.VMEM((3 * DN_HEADS, tm, LANES), F32),
                        pltpu.VMEM((3 * DN_HEADS, tm, LANES), F32)],
        compiler_params=pltpu.CompilerParams(dimension_semantics=("arbitrary",),
                                             vmem_limit_bytes=VMEM_LIMIT),
        name="inproj",
    )(x, g, w, conv_w)


def _deltanet_kernel(qkv_ref, z_ref, ba_ref, alog_ref, dtb_ref, ng_ref, o_ref,
                     state_ref, u_scr, wq_scr, ai_scr, kd_scr, el_scr):
    C = DN_CHUNK
    B = qkv_ref.shape[0]
    s = pl.program_id(0)

    @pl.when(s == 0)
    def _():
        u_scr[...] = jnp.zeros_like(u_scr)
        wq_scr[...] = jnp.zeros_like(wq_scr)
        ai_scr[...] = jnp.zeros_like(ai_scr)
        kd_scr[...] = jnp.zeros_like(kd_scr)
        el_scr[...] = jnp.zeros_like(el_scr)
        state_ref[...] = jnp.zeros_like(state_ref)

    lane = lax.broadcasted_iota(jnp.int32, (C, LANES), 1)
    row = lax.broadcasted_iota(jnp.int32, (C, LANES), 0)
    causal = row >= lane
    strict = row > lane
    eye = jnp.where(row == lane, 1.0, 0.0)
    sib = row ^ lane
    BH = [(b, h) for b in range(B) for h in range(DN_HEADS)]
    N = range(len(BH))

    def lane_bcast(x, j):
        return jnp.broadcast_to(x[:, j:j + 1], (x.shape[0], LANES))

    state = [state_ref[i] for i in N]
    el = [lane_bcast(el_scr[b, 0:1, :], DN_HEADS + h) for b, h in BH]
    ws = [_bdot(wq_scr[i], state[i]) for i in N]

    tril = jnp.where(causal, 1.0, 0.0).astype(BF16)
    gc_all, gc_all_t, beta_all, eg_all, ed_all, el_new = [], [], [], [], [], []
    for b in range(B):
        ba = ba_ref[b]
        xg = ba + dtb_ref[...]
        softplus = jnp.maximum(xg, 0.0) + jnp.log(1.0 + jnp.exp(-jnp.abs(xg)))
        g_all = -jnp.exp(alog_ref[...]) * softplus
        g_hi = g_all.astype(BF16)
        r1 = g_all - g_hi.astype(F32)
        g_mid = r1.astype(BF16)
        g_lo = (r1 - g_mid.astype(F32)).astype(BF16)
        gc = (jnp.dot(tril, g_hi, preferred_element_type=F32)
              + jnp.dot(tril, g_mid, preferred_element_type=F32)
              + jnp.dot(tril, g_lo, preferred_element_type=F32))
        g_last = gc[C - 1:C, :]
        gc_all.append(gc)
        gc_all_t.append(gc.T)
        beta_all.append(_sigmoid(ba).astype(BF16))
        eg_all.append(jnp.exp(gc).astype(BF16))
        ed_all.append(jnp.exp(g_last - gc).astype(BF16))
        el_new.append(jnp.exp(g_last))

    q = [qkv_ref[b, :, h * DN_DK:(h + 1) * DN_DK] for b, h in BH]
    k = [qkv_ref[b, :, DN_W + h * DN_DK:DN_W + (h + 1) * DN_DK] for b, h in BH]
    v = [qkv_ref[b, :, 2 * DN_W + h * DN_DV:2 * DN_W + (h + 1) * DN_DV] for b, h in BH]
    beta = [lane_bcast(beta_all[b], h) for b, h in BH]
    eg = [lane_bcast(eg_all[b], DN_HEADS + h) for b, h in BH]
    ed = [lane_bcast(ed_all[b], DN_HEADS + h) for b, h in BH]
    kb = [k[i] * beta[i] for i in N]
    kk = [_bdot_nt(kb[i], k[i]) for i in N]
    qk = [_bdot_nt(q[i], k[i]) for i in N]

    v_new = [u_scr[i] - ws[i][:C] for i in N]
    av = [_bdot(ai_scr[i], v_new[i]) for i in N]
    kv = [_bdot_tn(kd_scr[i], v_new[i]) for i in N]

    gc = [lane_bcast(gc_all[b], DN_HEADS + h) for b, h in BH]
    gc_t = [jnp.broadcast_to(gc_all_t[b][DN_HEADS + h:DN_HEADS + h + 1, :], (C, LANES))
            for b, h in BH]
    decay = [jnp.where(causal, jnp.exp(jnp.where(causal, gc[i] - gc_t[i], 0.0)), 0.0) for i in N]
    a_mat = [jnp.where(strict, kk[i] * decay[i], 0.0) for i in N]
    a_intra = [jnp.where(causal, qk[i] * decay[i], 0.0).astype(BF16) for i in N]
    t_mat = [eye - jnp.where(sib < 2, a_mat[i], 0.0) for i in N]

    def level(size, t_mat):
        off = (sib >= size) & (sib < 2 * size)
        if size < 8:
            x_mat = [_bdot(jnp.where(off, a_mat[i], 0.0), t_mat[i]) for i in N]
            return [t_mat[i] - _bdot(t_mat[i], x_mat[i]) for i in N]
        odd = [r for r in range(0, C, size) if (r // size) % 2 == 1]
        rows = lambda m: jnp.concatenate([m[r:r + size] for r in odd], axis=0)
        off_rows = rows(jnp.where(off, 1.0, 0.0)) > 0.5
        zero = jnp.zeros((size, C), F32)
        x_half = [_bdot(jnp.where(off_rows, rows(a_mat[i]), 0.0), t_mat[i]) for i in N]
        t_rows = [rows(t_mat[i]) for i in N]
        x_full = [jnp.concatenate(
            [x_half[i][(r // (2 * size)) * size:(r // (2 * size) + 1) * size]
             if r in odd else zero for r in range(0, C, size)], axis=0) for i in N]
        new_rows = [t_rows[i] - _bdot(t_rows[i], x_full[i]) for i in N]
        return [jnp.concatenate(
            [new_rows[i][(r // (2 * size)) * size:(r // (2 * size) + 1) * size]
             if r in odd else t_mat[i][r:r + size] for r in range(0, C, size)], axis=0)
            for i in N]

    t_mat = level(2, t_mat)

    for i, (b, h) in enumerate(BH):
        state_ref[i] = state[i] * el[i] + kv[i]
        o_i = ws[i][C:] + av[i]
        o_ref[b, :, h * DN_DV:(h + 1) * DN_DV] = (
            _rms(o_i, ng_ref[...]) * z_ref[b, :, h * DN_DV:(h + 1) * DN_DV].astype(F32)
        ).astype(o_ref.dtype)

    size = 4
    while size < C:
        t_mat = level(size, t_mat)
        size *= 2
    uw = [_bdot(t_mat[i], jnp.concatenate([v[i] * beta[i], kb[i] * eg[i]], axis=1))
          for i in N]
    for i, (b, h) in enumerate(BH):
        u_scr[i] = uw[i][:, :DN_DV]
        wq_scr[i, 0:C, :] = uw[i][:, DN_DV:].astype(BF16)
        wq_scr[i, C:2 * C, :] = q[i] * eg[i]
        ai_scr[i] = a_intra[i]
        kd_scr[i] = k[i] * ed[i]
    for b in range(B):
        el_scr[b] = jnp.broadcast_to(el_new[b], el_scr.shape[1:])


def _deltanet(qkv, z, ba, alog_row, dtb_row, norm_g, batch, seq):
    C = DN_CHUNK
    nc = seq // C
    n_bh = batch * DN_HEADS
    cur = lambda s: (0, jnp.minimum(s, nc - 1), 0)
    prev = lambda s: (0, jnp.maximum(s - 1, 0), 0)
    const = lambda s: (0, 0)
    out = pl.pallas_call(
        _deltanet_kernel,
        grid=(nc + 1,),
        in_specs=[pl.BlockSpec((batch, C, 3 * DN_W), cur),
                  pl.BlockSpec((batch, C, DN_W), prev),
                  pl.BlockSpec((batch, C, LANES), cur),
                  pl.BlockSpec((1, LANES), const), pl.BlockSpec((1, LANES), const),
                  pl.BlockSpec((1, LANES), const)],
        out_specs=pl.BlockSpec((batch, C, DN_W), prev),
        out_shape=jax.ShapeDtypeStruct((batch, seq, DN_W), BF16),
        scratch_shapes=[pltpu.VMEM((n_bh, DN_DK, DN_DV), F32),
                        pltpu.VMEM((n_bh, C, DN_DV), F32),
                        pltpu.VMEM((n_bh, 2 * C, DN_DK), BF16),
                        pltpu.VMEM((n_bh, C, C), BF16),
                        pltpu.VMEM((n_bh, C, DN_DK), BF16),
                        pltpu.VMEM((batch, 8, LANES), F32)],
        compiler_params=pltpu.CompilerParams(dimension_semantics=("arbitrary",),
                                             vmem_limit_bytes=VMEM_LIMIT),
        name="deltanet",
    )(qkv.reshape(batch, seq, 3 * DN_W), z.reshape(batch, seq, DN_W),
      ba.reshape(batch, seq, LANES), alog_row, dtb_row, norm_g)
    return out.reshape(batch * seq, DN_W)


def _rope_table_kernel(pos_ref, freq_ref, place_ref, c_ref, s1_ref, s2_ref):
    ang = freq_ref[...] * pos_ref[0].astype(F32)
    for out_ref, val, k in ((c_ref, jnp.cos(ang) - 1.0, 0), (s1_ref, -jnp.sin(ang), 1),
                            (s2_ref, jnp.sin(ang), 2)):
        hi = val.astype(BF16)
        r1 = val - hi.astype(F32)
        mid = r1.astype(BF16)
        lo = (r1 - mid.astype(F32)).astype(BF16)
        place = place_ref[k]
        acc = sum(lax.dot_general(t, place, (((0,), (0,)), ((), ())),
                                  preferred_element_type=F32) for t in (hi, mid, lo))
        out_ref[...] = acc + 1.0 if k == 0 else acc


def _rope_tables(positions, tm=2048):
    t = positions.size
    half = ROT_DIM // 2
    lane = np.arange(LANES)
    within = lane % SW_HEAD_DIM
    inv_freq = ROPE_THETA ** (-np.arange(half, dtype=np.float32) * (2.0 / ROT_DIM))
    onehot = (within[None, :] % half == np.arange(half)[:, None])
    place = np.stack([onehot & (within < ROT_DIM)[None, :],
                      onehot & (within < half)[None, :],
                      onehot & ((within >= half) & (within < ROT_DIM))[None, :]]).astype(np.float32)
    out = jax.ShapeDtypeStruct((t, LANES), F32)
    return pl.pallas_call(
        _rope_table_kernel,
        grid=(t // tm,),
        in_specs=[pl.BlockSpec((None, 1, tm), lambda i: (i, 0, 0)),
                  pl.BlockSpec((half, 1), lambda i: (0, 0)),
                  pl.BlockSpec((3, half, LANES), lambda i: (0, 0, 0))],
        out_specs=[pl.BlockSpec((tm, LANES), lambda i: (i, 0))] * 3,
        out_shape=[out, out, out],
        name="rope_tables",
    )(positions.reshape(t // tm, 1, tm), jnp.asarray(inv_freq.reshape(half, 1)),
      jnp.asarray(place, dtype=BF16))


def _swa_kernel(sink_ref, q_ref, kv_ref, kvn_ref, c_ref, s1_ref, s2_ref, cn_ref, s1n_ref, s2n_ref,
                o_ref, k_scr, v_scr, *, blocks_per_seq):
    Q = SW_BLOCK
    step = pl.program_id(0)
    half = SW_HEAD_DIM
    lane = lax.broadcasted_iota(jnp.int32, (Q, LANES), 1)
    low = lane < half
    log2e = 1.4426950408889634

    def rope(x, cos, sin1, sin2):
        return x * cos + pltpu.roll(x, LANES - ROT_DIM // 2, 1) * sin1 \
            + pltpu.roll(x, ROT_DIM // 2, 1) * sin2

    def prepare(kv_tile, cos, sin1, sin2, dst):
        kv = kv_tile.astype(F32)
        kr = rope(kv[:, :LANES], cos, sin1, sin2)
        kr_sw = pltpu.roll(kr, half, 1)
        vt = kv[:, LANES:]
        vt_sw = pltpu.roll(vt, half, 1)
        ones_lo = jnp.where(lane == half, 1.0, 0.0)
        ones_hi = jnp.where(lane == 0, 1.0, 0.0)
        k_scr[0, dst, :] = jnp.where(low, kr, kr_sw).astype(BF16)
        k_scr[1, dst, :] = jnp.where(low, kr_sw, kr).astype(BF16)
        v_scr[0, dst, :] = jnp.where(low, vt, ones_lo).astype(BF16)
        v_scr[1, dst, :] = jnp.where(low, ones_hi, vt_sw).astype(BF16)
        v_scr[2, dst, :] = jnp.where(low, vt_sw, ones_lo).astype(BF16)
        v_scr[3, dst, :] = jnp.where(low, ones_hi, vt).astype(BF16)

    @pl.when(step == 0)
    def _():
        k_scr[...] = jnp.zeros_like(k_scr)
        v_scr[...] = jnp.zeros_like(v_scr)
        prepare(kv_ref[...], c_ref[...], s1_ref[...], s2_ref[...], slice(Q, 2 * Q))

    @pl.when(step > 0)
    def _():
        for j in range(SW_KV_HEADS):
            k_scr[j, 0:2 * Q, :] = k_scr[j, Q:3 * Q, :]
        for j in range(2 * SW_KV_HEADS):
            v_scr[j, 0:2 * Q, :] = v_scr[j, Q:3 * Q, :]

    first_of_seq = (step % blocks_per_seq) == 0
    qi = lax.broadcasted_iota(jnp.int32, (Q, 2 * Q), 0)
    ki = lax.broadcasted_iota(jnp.int32, (Q, 2 * Q), 1)
    valid = (ki > qi) & (ki <= qi + Q) & ((ki >= Q) | jnp.logical_not(first_of_seq))
    cos = c_ref[...]
    sin1 = s1_ref[...]
    sin2 = s2_ref[...]

    pairs_per_kv = SW_Q_HEADS // SW_KV_HEADS // 2
    heads = []
    for pair in range(SW_Q_HEADS // 2):
        heads += [(pair, pair // pairs_per_kv, False), (pair, pair // pairs_per_kv, True)]
    qp = [rope(q_ref[:, pair * LANES:(pair + 1) * LANES].astype(F32), cos, sin1, sin2)
          * (SW_HEAD_DIM ** -0.5 * log2e) for pair in range(SW_Q_HEADS // 2)]
    s = [_bdot_nt(jnp.where(low, 0.0, qp[pair]) if hi else jnp.where(low, qp[pair], 0.0),
                  k_scr[j, 0:2 * Q, :]) for pair, j, hi in heads]
    sink = [sink_ref[2 * pair + int(hi)] * log2e for pair, j, hi in heads]
    s = [jnp.where(valid, t, -jnp.inf) for t in s]
    m = [jnp.maximum(jnp.max(s[i], axis=-1, keepdims=True), sink[i]) for i in range(len(heads))]
    p = [jnp.exp2(s[i] - m[i]) for i in range(len(heads))]
    pv = [_bdot(p[i], v_scr[2 * j + int(hi), 0:2 * Q, :]) for i, (pair, j, hi) in enumerate(heads)]
    prepare(kvn_ref[...], cn_ref[...], s1n_ref[...], s2n_ref[...], slice(2 * Q, 3 * Q))
    out = []
    for i, (pair, j, hi) in enumerate(heads):
        sum_lane = 0 if hi else half
        denom = pv[i][:, sum_lane:sum_lane + 1] + jnp.exp2(sink[i] - m[i])
        out.append(pv[i] * (1.0 / denom))
    for i in range(0, len(heads), 2):
        pair = heads[i][0]
        o_ref[:, pair * LANES:(pair + 1) * LANES] = jnp.where(
            low, out[i], out[i + 1]).astype(o_ref.dtype)


def _swa(sw, sinks, cos_t, sin1_t, sin2_t, batch, seq):
    t = sw.shape[0]
    Q = SW_BLOCK
    total = t // Q
    blk = lambda i, s: (i, 0)
    nxt = lambda i, s: (jnp.minimum(i + 1, total - 1), 0)
    kv_col = SW_Q_W // (2 * SW_KV_W)
    return pl.pallas_call(
        functools.partial(_swa_kernel, blocks_per_seq=seq // Q),
        grid_spec=pltpu.PrefetchScalarGridSpec(
            num_scalar_prefetch=1,
            grid=(total,),
            in_specs=[pl.BlockSpec((Q, SW_Q_W), blk),
                      pl.BlockSpec((Q, 2 * SW_KV_W), lambda i, s: (i, kv_col)),
                      pl.BlockSpec((Q, 2 * SW_KV_W),
                                   lambda i, s: (jnp.minimum(i + 1, total - 1), kv_col)),
                      pl.BlockSpec((Q, LANES), blk), pl.BlockSpec((Q, LANES), blk),
                      pl.BlockSpec((Q, LANES), blk),
                      pl.BlockSpec((Q, LANES), nxt), pl.BlockSpec((Q, LANES), nxt),
                      pl.BlockSpec((Q, LANES), nxt)],
            out_specs=pl.BlockSpec((Q, SW_Q_W), blk),
            scratch_shapes=[pltpu.VMEM((SW_KV_HEADS, 3 * Q, LANES), BF16),
                            pltpu.VMEM((2 * SW_KV_HEADS, 3 * Q, LANES), BF16)]),
        out_shape=jax.ShapeDtypeStruct((t, SW_Q_W), BF16),
        compiler_params=pltpu.CompilerParams(dimension_semantics=("arbitrary",),
                                             vmem_limit_bytes=VMEM_LIMIT),
        name="swa",
    )(sinks, sw, sw, sw, cos_t, sin1_t, sin2_t, cos_t, sin1_t, sin2_t)


def _merge_kernel(x_ref, odn_ref, osw_ref, gate_ref, wdn_ref, wsw_ref, wo_ref, g_ref, out_ref):
    tm = x_ref.shape[0]
    subs = [slice(r, r + MERGE_SUB) for r in range(0, tm, MERGE_SUB)]

    def up(rows):
        return (jnp.dot(odn_ref[rows, :], wdn_ref[...], preferred_element_type=F32),
                jnp.dot(osw_ref[rows, :], wsw_ref[...], preferred_element_type=F32))

    def finish(rows, y_a, y_b):
        ga = gate_ref[rows, :D_MODEL].astype(F32)
        gb = gate_ref[rows, D_MODEL:].astype(F32)
        mix = _sigmoid(ga) * y_a + _sigmoid(gb) * y_b
        y = jnp.dot(mix.astype(BF16), wo_ref[...], preferred_element_type=F32)
        out_ref[rows, :] = x_ref[rows, :] + _rms(y, g_ref[...])

    pending = up(subs[0])
    for i, rows in enumerate(subs):
        nxt = up(subs[i + 1]) if i + 1 < len(subs) else None
        finish(rows, *pending)
        pending = nxt


def _merge(x, o_dn, o_sw, gates, w_dn, w_sw, w_o, layer, g, tm=1024):
    t = x.shape[0]
    row = lambda i: (i, 0)
    const = lambda i: (0, 0)
    wspec = pl.BlockSpec((None, D_MODEL, D_MODEL), lambda i: (layer, 0, 0))
    return pl.pallas_call(
        _merge_kernel,
        grid=(t // tm,),
        in_specs=[pl.BlockSpec((tm, D_MODEL), row), pl.BlockSpec((tm, DN_W), row),
                  pl.BlockSpec((tm, SW_Q_W), row), pl.BlockSpec((tm, GATE_COLS), row),
                  wspec, wspec, wspec, pl.BlockSpec((1, D_MODEL), const)],
        out_specs=pl.BlockSpec((tm, D_MODEL), row),
        out_shape=jax.ShapeDtypeStruct((t, D_MODEL), F32),
        compiler_params=pltpu.CompilerParams(dimension_semantics=("arbitrary",),
                                             vmem_limit_bytes=VMEM_LIMIT),
        name="merge",
    )(x, o_dn, o_sw, gates, w_dn, w_sw, w_o, g)


def _mlp_kernel(x_ref, gpre_ref, w1_ref, w2_ref, gpost_ref, out_ref):
    tm = x_ref.shape[0]
    subs = [slice(r, r + MLP_SUB) for r in range(0, tm, MLP_SUB)]

    def hidden(rows):
        return _rms(x_ref[rows, :], gpre_ref[...]).astype(BF16)

    def finish(rows, acc):
        out_ref[rows, :] = x_ref[rows, :] + _rms(acc, gpost_ref[...])

    h = hidden(subs[0])
    done = None
    for i, rows in enumerate(subs):
        acc = None
        h_next = None
        for f in range(0, D_FF, MLP_FF_CHUNK):
            a = jnp.maximum(jnp.dot(h, w1_ref[:, f:f + MLP_FF_CHUNK],
                                    preferred_element_type=F32), 0.0)
            t = jnp.dot((a * a).astype(BF16), w2_ref[f:f + MLP_FF_CHUNK, :],
                        preferred_element_type=F32)
            acc = t if acc is None else acc + t
            if f == 0:
                if i + 1 < len(subs):
                    h_next = hidden(subs[i + 1])
                if done is not None:
                    finish(*done)
        done = (rows, acc)
        h = h_next
    finish(*done)


def _mlp(x, g_pre, w1, w2, layer, g_post, tm=1024):
    t = x.shape[0]
    row = lambda i: (i, 0)
    const = lambda i: (0, 0)
    return pl.pallas_call(
        _mlp_kernel,
        grid=(t // tm,),
        in_specs=[pl.BlockSpec((tm, D_MODEL), row), pl.BlockSpec((1, D_MODEL), const),
                  pl.BlockSpec((None, D_MODEL, D_FF), lambda i: (layer, 0, 0),
                               pipeline_mode=pl.Buffered(1)),
                  pl.BlockSpec((None, D_FF, D_MODEL), lambda i: (layer, 0, 0),
                               pipeline_mode=pl.Buffered(1)),
                  pl.BlockSpec((1, D_MODEL), const)],
        out_specs=pl.BlockSpec((tm, D_MODEL), row),
        out_shape=jax.ShapeDtypeStruct((t, D_MODEL), F32),
        compiler_params=pltpu.CompilerParams(dimension_semantics=("arbitrary",),
                                             vmem_limit_bytes=VMEM_LIMIT),
        name="mlp",
    )(x, g_pre, w1, w2, g_post)


def _pack_w_in(w_in):
    o = 0
    dn = w_in[..., o:o + DN_COLS]
    o += DN_COLS
    ba = w_in[..., o:o + 2 * DN_HEADS]
    o += 2 * DN_HEADS
    sw = w_in[..., o:o + SW_COLS]
    o += SW_COLS
    gates = w_in[..., o:o + GATE_COLS]
    ba = jnp.pad(ba, [(0, 0)] * (ba.ndim - 1) + [(0, BA_COLS - 2 * DN_HEADS)])
    return jnp.concatenate([dn, sw, gates, ba], axis=-1).astype(BF16)


def _lane_row(vec, offset):
    return jnp.zeros((1, LANES), F32).at[0, offset:offset + vec.shape[0]].set(vec.astype(F32))


def kernel(x, positions, pre_mix_g, w_in, dn_conv_w, dn_a_log, dn_dt_bias, dn_norm_g, sw_sinks,
           w_up_dn, w_up_sw, w_o, post_mix_g, pre_mlp_g, w_ff1, w_ff2, post_mlp_g):
    batch, seq, _ = x.shape
    depth = w_in.shape[0]
    t = batch * seq
    xf = x.reshape(t, D_MODEL)
    cos_t, sin1_t, sin2_t = _rope_tables(positions)
    w_in_p = _pack_w_in(w_in)
    w_dn, w_sw, w_out = w_up_dn.astype(BF16), w_up_sw.astype(BF16), w_o.astype(BF16)
    w1, w2 = w_ff1.astype(BF16), w_ff2.astype(BF16)
    for l in range(depth):
        qkv, z, sw, gates, ba = _inproj(xf, pre_mix_g[l].reshape(1, D_MODEL), w_in_p, l,
                                        dn_conv_w[l], seq)
        o_dn = _deltanet(qkv, z, ba, _lane_row(dn_a_log[l], DN_HEADS),
                         _lane_row(dn_dt_bias[l], DN_HEADS), dn_norm_g[l].reshape(1, DN_DV),
                         batch, seq)
        o_sw = _swa(sw, sw_sinks[l].astype(F32), cos_t, sin1_t, sin2_t, batch, seq)
        xf = _merge(xf, o_dn, o_sw, gates, w_dn, w_sw, w_out, l,
                    post_mix_g[l].reshape(1, D_MODEL))
        xf = _mlp(xf, pre_mlp_g[l].reshape(1, D_MODEL), w1, w2, l,
                  post_mlp_g[l].reshape(1, D_MODEL))
    return xf.reshape(batch, seq, D_MODEL)
```

```python
import functools

import numpy as np
import jax
import jax.numpy as jnp
from jax import lax
from jax.experimental import pallas as pl
from jax.experimental.pallas import tpu as pltpu

F32 = jnp.float32
BF16 = jnp.bfloat16

D_MODEL = 1024
DN_HEADS = 8
DN_DK = 128
DN_DV = 128
DN_CONV = 4
SW_Q_HEADS = 16
SW_KV_HEADS = 2
SW_HEAD_DIM = 64
SW_BLOCK = 128
SW_STEP_BLOCKS = 2
ROPE_THETA = 500000.0
ROT_DIM = SW_HEAD_DIM // 4
D_FF = 4 * D_MODEL
EPS = 1e-6

DN_W = DN_HEADS * DN_DK
SW_Q_W = SW_Q_HEADS * SW_HEAD_DIM
SW_KV_W = SW_KV_HEADS * SW_HEAD_DIM
LANES = 128

DN_COLS = 4 * DN_W
SW_COLS = SW_Q_W + 2 * SW_KV_W
GATE_COLS = 2 * D_MODEL
BA_COLS = LANES
PACKED_COLS = DN_COLS + SW_COLS + GATE_COLS + BA_COLS

DN_CHUNK = 128
MLP_SUB = 256
MLP_FF_CHUNK = 1024
MERGE_SUB = 256
INPROJ_CHUNK = 256
VMEM_LIMIT = 56 * 1024 * 1024


def _bdot(a, b):
    return jnp.dot(a.astype(BF16), b.astype(BF16), preferred_element_type=F32)


def _bdot_nt(a, b):
    return lax.dot_general(a.astype(BF16), b.astype(BF16), (((1,), (1,)), ((), ())),
                           preferred_element_type=F32)


def _bdot_tn(a, b):
    return lax.dot_general(a.astype(BF16), b.astype(BF16), (((0,), (0,)), ((), ())),
                           preferred_element_type=F32)


def _rms(x, g):
    return x * lax.rsqrt(jnp.mean(x * x, axis=-1, keepdims=True) + EPS) * g


def _sigmoid(x):
    return 1.0 / (1.0 + jnp.exp(-x))


def _inproj_kernel(x_ref, g_ref, w_ref, cw_ref, qkv_ref, z_ref, sw_ref, gate_ref, ba_ref,
                   tail_ref, pbuf_ref, ybuf_ref, *, tiles_per_seq):
    tm = x_ref.shape[0]

    @pl.when(pl.program_id(0) % tiles_per_seq == 0)
    def _():
        tail_ref[...] = jnp.zeros_like(tail_ref)

    h = _rms(x_ref[...], g_ref[...]).astype(BF16)

    def proj(off, width):
        return jnp.dot(h, w_ref[:, off:off + width], preferred_element_type=F32)

    def conv_matmul(c):
        p = proj(c, INPROJ_CHUNK)
        for hh in range(0, INPROJ_CHUNK, DN_DK):
            pbuf_ref[(c + hh) // DN_DK] = p[:, hh:hh + DN_DK]

    def conv_epilogue(c):
        rows = tm // 8
        for hh in range(0, INPROJ_CHUNK, DN_DK):
            g = (c + hh) // DN_DK
            cols = slice(c + hh, c + hh + DN_DK)
            x = [pbuf_ref[g, pl.ds(j, rows, stride=8), :] for j in range(8)]
            prev = {}
            for j in range(8 - (DN_CONV - 1), 8):
                prev[j - 8] = jnp.concatenate([tail_ref[j:j + 1, cols], x[j][:rows - 1, :]],
                                              axis=0)
            tail_ref[:, cols] = pbuf_ref[g, tm - 8:tm, :]
            tok = lambda j: x[j] if j >= 0 else prev[j]
            for j in range(8):
                y = x[j] * cw_ref[DN_CONV - 1:DN_CONV, cols]
                for i in range(DN_CONV - 1):
                    y = y + tok(j - (DN_CONV - 1 - i)) * cw_ref[i:i + 1, cols]
                y = y * _sigmoid(y)
                if c + hh < 2 * DN_W:
                    scale = DN_DK ** -0.5 if c + hh < DN_W else 1.0
                    y = y * (lax.rsqrt(jnp.sum(y * y, axis=-1, keepdims=True) + EPS) * scale)
                ybuf_ref[g, pl.ds(j, rows, stride=8), :] = y
            qkv_ref[:, cols] = ybuf_ref[g].astype(qkv_ref.dtype)

    def silu_chunk(c):
        zc = proj(3 * DN_W + c, INPROJ_CHUNK)
        z_ref[:, c:c + INPROJ_CHUNK] = (zc * _sigmoid(zc)).astype(z_ref.dtype)

    def plain_chunk(ref, off, c, w):
        ref[:, c:c + w] = proj(off + c, w).astype(ref.dtype)

    conv_cols = list(range(0, 3 * DN_W, INPROJ_CHUNK))
    light = [functools.partial(silu_chunk, c) for c in range(0, DN_W, INPROJ_CHUNK)]
    off = DN_COLS
    for ref, width in ((sw_ref, SW_COLS), (gate_ref, GATE_COLS), (ba_ref, BA_COLS)):
        for c in range(0, width, INPROJ_CHUNK):
            light.append(functools.partial(plain_chunk, ref, off, c, min(INPROJ_CHUNK, width - c)))
        off += width
    per_conv = -(-len(light) // len(conv_cols))
    conv_matmul(conv_cols[0])
    for i, c in enumerate(conv_cols):
        for fn in light[i * per_conv:(i + 1) * per_conv]:
            fn()
        if i + 1 < len(conv_cols):
            conv_matmul(conv_cols[i + 1])
        conv_epilogue(c)


def _inproj(x, g, w, layer, conv_w, seq, tm=256):
    t = x.shape[0]
    row = lambda i: (i, 0)
    const = lambda i: (0, 0)
    return pl.pallas_call(
        functools.partial(_inproj_kernel, tiles_per_seq=seq // tm),
        grid=(t // tm,),
        in_specs=[pl.BlockSpec((tm, D_MODEL), row),
                  pl.BlockSpec((1, D_MODEL), const),
                  pl.BlockSpec((None, D_MODEL, PACKED_COLS), lambda i: (layer, 0, 0),
                               pipeline_mode=pl.Buffered(1)),
                  pl.BlockSpec((DN_CONV, 3 * DN_W), const)],
        out_specs=[pl.BlockSpec((tm, 3 * DN_W), row), pl.BlockSpec((tm, DN_W), row),
                   pl.BlockSpec((tm, SW_COLS), row),
                   pl.BlockSpec((tm, GATE_COLS), row), pl.BlockSpec((tm, BA_COLS), row)],
        out_shape=[jax.ShapeDtypeStruct((t, 3 * DN_W), BF16),
                   jax.ShapeDtypeStruct((t, DN_W), BF16),
                   jax.ShapeDtypeStruct((t, SW_COLS), BF16),
                   jax.ShapeDtypeStruct((t, GATE_COLS), BF16),
                   jax.ShapeDtypeStruct((t, BA_COLS), F32)],
        scratch_shapes=[pltpu.VMEM((8, 3 * DN_W), F32), pltpu.VMEM((3 * DN_HEADS, tm, LANES), F32),
                        pltpu.VMEM((3 * DN_HEADS, tm, LANES), F32)],
        compiler_params=pltpu.CompilerParams(dimension_semantics=("arbitrary",),
                                             vmem_limit_bytes=VMEM_LIMIT),
        name="inproj",
    )(x, g, w, conv_w)


def _deltanet_kernel(qkv_ref, z_ref, ba_ref, alog_ref, dtb_ref, ng_ref, o_ref,
                     state_ref, u_scr, wq_scr, ai_scr, kd_scr, el_scr):
    C = DN_CHUNK
    B = qkv_ref.shape[0]
    s = pl.program_id(0)

    @pl.when(s == 0)
    def _():
        u_scr[...] = jnp.zeros_like(u_scr)
        wq_scr[...] = jnp.zeros_like(wq_scr)
        ai_scr[...] = jnp.zeros_like(ai_scr)
        kd_scr[...] = jnp.zeros_like(kd_scr)
        el_scr[...] = jnp.zeros_like(el_scr)
        state_ref[...] = jnp.zeros_like(state_ref)

    lane = lax.broadcasted_iota(jnp.int32, (C, LANES), 1)
    row = lax.broadcasted_iota(jnp.int32, (C, LANES), 0)
    causal = row >= lane
    strict = row > lane
    eye = jnp.where(row == lane, 1.0, 0.0)
    sib = row ^ lane
    BH = [(b, h) for b in range(B) for h in range(DN_HEADS)]
    N = range(len(BH))

    def lane_bcast(x, j):
        return jnp.broadcast_to(x[:, j:j + 1], (x.shape[0], LANES))

    state = [state_ref[i] for i in N]
    el = [lane_bcast(el_scr[b, 0:1, :], DN_HEADS + h) for b, h in BH]
    ws = [_bdot(wq_scr[i], state[i]) for i in N]

    tril = jnp.where(causal, 1.0, 0.0).astype(BF16)
    gc_all, gc_all_t, beta_all, eg_all, ed_all, el_new = [], [], [], [], [], []
    for b in range(B):
        ba = ba_ref[b]
        xg = ba + dtb_ref[...]
        softplus = jnp.maximum(xg, 0.0) + jnp.log(1.0 + jnp.exp(-jnp.abs(xg)))
        g_all = -jnp.exp(alog_ref[...]) * softplus
        g_hi = g_all.astype(BF16)
        r1 = g_all - g_hi.astype(F32)
        g_mid = r1.astype(BF16)
        g_lo = (r1 - g_mid.astype(F32)).astype(BF16)
        gc = (jnp.dot(tril, g_hi, preferred_element_type=F32)
              + jnp.dot(tril, g_mid, preferred_element_type=F32)
              + jnp.dot(tril, g_lo, preferred_element_type=F32))
        g_last = gc[C - 1:C, :]
        gc_all.append(gc)
        gc_all_t.append(gc.T)
        beta_all.append(_sigmoid(ba).astype(BF16))
        eg_all.append(jnp.exp(gc).astype(BF16))
        ed_all.append(jnp.exp(g_last - gc).astype(BF16))
        el_new.append(jnp.exp(g_last))

    q = [qkv_ref[b, :, h * DN_DK:(h + 1) * DN_DK] for b, h in BH]
    k = [qkv_ref[b, :, DN_W + h * DN_DK:DN_W + (h + 1) * DN_DK] for b, h in BH]
    v = [qkv_ref[b, :, 2 * DN_W + h * DN_DV:2 * DN_W + (h + 1) * DN_DV] for b, h in BH]
    beta = [lane_bcast(beta_all[b], h) for b, h in BH]
    eg = [lane_bcast(eg_all[b], DN_HEADS + h) for b, h in BH]
    ed = [lane_bcast(ed_all[b], DN_HEADS + h) for b, h in BH]
    kb = [k[i] * beta[i] for i in N]
    kk = [_bdot_nt(kb[i], k[i]) for i in N]
    qk = [_bdot_nt(q[i], k[i]) for i in N]

    v_new = [u_scr[i] - ws[i][:C] for i in N]
    av = [_bdot(ai_scr[i], v_new[i]) for i in N]
    kv = [_bdot_tn(kd_scr[i], v_new[i]) for i in N]

    gc = [lane_bcast(gc_all[b], DN_HEADS + h) for b, h in BH]
    gc_t = [jnp.broadcast_to(gc_all_t[b][DN_HEADS + h:DN_HEADS + h + 1, :], (C, LANES))
            for b, h in BH]
    decay = [jnp.where(causal, jnp.exp(jnp.where(causal, gc[i] - gc_t[i], 0.0)), 0.0) for i in N]
    a_mat = [jnp.where(strict, kk[i] * decay[i], 0.0) for i in N]
    a_intra = [jnp.where(causal, qk[i] * decay[i], 0.0).astype(BF16) for i in N]
    t_mat = [eye - jnp.where(sib < 2, a_mat[i], 0.0) for i in N]

    def level(size, t_mat):
        off = (sib >= size) & (sib < 2 * size)
        if size < 8:
            x_mat = [_bdot(jnp.where(off, a_mat[i], 0.0), t_mat[i]) for i in N]
            return [t_mat[i] - _bdot(t_mat[i], x_mat[i]) for i in N]
        odd = [r for r in range(0, C, size) if (r // size) % 2 == 1]
        rows = lambda m: jnp.concatenate([m[r:r + size] for r in odd], axis=0)
        off_rows = rows(jnp.where(off, 1.0, 0.0)) > 0.5
        zero = jnp.zeros((size, C), F32)
        x_half = [_bdot(jnp.where(off_rows, rows(a_mat[i]), 0.0), t_mat[i]) for i in N]
        t_rows = [rows(t_mat[i]) for i in N]
        x_full = [jnp.concatenate(
            [x_half[i][(r // (2 * size)) * size:(r // (2 * size) + 1) * size]
             if r in odd else zero for r in range(0, C, size)], axis=0) for i in N]
        new_rows = [t_rows[i] - _bdot(t_rows[i], x_full[i]) for i in N]
        return [jnp.concatenate(
            [new_rows[i][(r // (2 * size)) * size:(r // (2 * size) + 1) * size]
             if r in odd else t_mat[i][r:r + size] for r in range(0, C, size)], axis=0)
            for i in N]

    t_mat = level(2, t_mat)

    for i, (b, h) in enumerate(BH):
        state_ref[i] = state[i] * el[i] + kv[i]
        o_i = ws[i][C:] + av[i]
        o_ref[b, :, h * DN_DV:(h + 1) * DN_DV] = (
            _rms(o_i, ng_ref[...]) * z_ref[b, :, h * DN_DV:(h + 1) * DN_DV].astype(F32)
        ).astype(o_ref.dtype)

    size = 4
    while size < C:
        t_mat = level(size, t_mat)
        size *= 2
    uw = [_bdot(t_mat[i], jnp.concatenate([v[i] * beta[i], kb[i] * eg[i]], axis=1))
          for i in N]
    for i, (b, h) in enumerate(BH):
        u_scr[i] = uw[i][:, :DN_DV]
        wq_scr[i, 0:C, :] = uw[i][:, DN_DV:].astype(BF16)
        wq_scr[i, C:2 * C, :] = q[i] * eg[i]
        ai_scr[i] = a_intra[i]
        kd_scr[i] = k[i] * ed[i]
    for b in range(B):
        el_scr[b] = jnp.broadcast_to(el_new[b], el_scr.shape[1:])


def _deltanet(qkv, z, ba, alog_row, dtb_row, norm_g, batch, seq):
    C = DN_CHUNK
    nc = seq // C
    n_bh = batch * DN_HEADS
    cur = lambda s: (0, jnp.minimum(s, nc - 1), 0)
    prev = lambda s: (0, jnp.maximum(s - 1, 0), 0)
    const = lambda s: (0, 0)
    out = pl.pallas_call(
        _deltanet_kernel,
        grid=(nc + 1,),
        in_specs=[pl.BlockSpec((batch, C, 3 * DN_W), cur),
                  pl.BlockSpec((batch, C, DN_W), prev),
                  pl.BlockSpec((batch, C, LANES), cur),
                  pl.BlockSpec((1, LANES), const), pl.BlockSpec((1, LANES), const),
                  pl.BlockSpec((1, LANES), const)],
        out_specs=pl.BlockSpec((batch, C, DN_W), prev),
        out_shape=jax.ShapeDtypeStruct((batch, seq, DN_W), BF16),
        scratch_shapes=[pltpu.VMEM((n_bh, DN_DK, DN_DV), F32),
                        pltpu.VMEM((n_bh, C, DN_DV), F32),
                        pltpu.VMEM((n_bh, 2 * C, DN_DK), BF16),
                        pltpu.VMEM((n_bh, C, C), BF16),
                        pltpu.VMEM((n_bh, C, DN_DK), BF16),
                        pltpu.VMEM((batch, 8, LANES), F32)],
        compiler_params=pltpu.CompilerParams(dimension_semantics=("arbitrary",),
                                             vmem_limit_bytes=VMEM_LIMIT),
        name="deltanet",
    )(qkv.reshape(batch, seq, 3 * DN_W), z.reshape(batch, seq, DN_W),
      ba.reshape(batch, seq, LANES), alog_row, dtb_row, norm_g)
    return out.reshape(batch * seq, DN_W)


def _rope_table_kernel(pos_ref, freq_ref, place_ref, c_ref, s1_ref, s2_ref):
    ang = freq_ref[...] * pos_ref[0].astype(F32)
    for out_ref, val, k in ((c_ref, jnp.cos(ang) - 1.0, 0), (s1_ref, -jnp.sin(ang), 1),
                            (s2_ref, jnp.sin(ang), 2)):
        hi = val.astype(BF16)
        r1 = val - hi.astype(F32)
        mid = r1.astype(BF16)
        lo = (r1 - mid.astype(F32)).astype(BF16)
        place = place_ref[k]
        acc = sum(lax.dot_general(t, place, (((0,), (0,)), ((), ())),
                                  preferred_element_type=F32) for t in (hi, mid, lo))
        out_ref[...] = acc + 1.0 if k == 0 else acc


def _rope_tables(positions, tm=2048):
    t = positions.size
    half = ROT_DIM // 2
    lane = np.arange(LANES)
    within = lane % SW_HEAD_DIM
    inv_freq = ROPE_THETA ** (-np.arange(half, dtype=np.float32) * (2.0 / ROT_DIM))
    onehot = (within[None, :] % half == np.arange(half)[:, None])
    place = np.stack([onehot & (within < ROT_DIM)[None, :],
                      onehot & (within < half)[None, :],
                      onehot & ((within >= half) & (within < ROT_DIM))[None, :]]).astype(np.float32)
    out = jax.ShapeDtypeStruct((t, LANES), F32)
    return pl.pallas_call(
        _rope_table_kernel,
        grid=(t // tm,),
        in_specs=[pl.BlockSpec((None, 1, tm), lambda i: (i, 0, 0)),
                  pl.BlockSpec((half, 1), lambda i: (0, 0)),
                  pl.BlockSpec((3, half, LANES), lambda i: (0, 0, 0))],
        out_specs=[pl.BlockSpec((tm, LANES), lambda i: (i, 0))] * 3,
        out_shape=[out, out, out],
        name="rope_tables",
    )(positions.reshape(t // tm, 1, tm), jnp.asarray(inv_freq.reshape(half, 1)),
      jnp.asarray(place, dtype=BF16))


def _swa_kernel(sink_ref, q_ref, kv_ref, c_ref, s1_ref, s2_ref, o_ref, k_scr, v_scr,
                *, steps_per_seq):
    Q = SW_BLOCK
    NB = SW_STEP_BLOCKS
    step = pl.program_id(0)
    first_of_seq = (step % steps_per_seq) == 0
    half = SW_HEAD_DIM
    lane = lax.broadcasted_iota(jnp.int32, (Q, LANES), 1)
    low = lane < half
    log2e = 1.4426950408889634

    def rope(x, r0):
        rows = slice(r0, r0 + Q)
        return x * c_ref[rows, :] + pltpu.roll(x, LANES - ROT_DIM // 2, 1) * s1_ref[rows, :] \
            + pltpu.roll(x, ROT_DIM // 2, 1) * s2_ref[rows, :]

    @pl.when(step == 0)
    def _():
        k_scr[...] = jnp.zeros_like(k_scr)
        v_scr[...] = jnp.zeros_like(v_scr)

    @pl.when(step > 0)
    def _():
        for j in range(SW_KV_HEADS):
            k_scr[j, 0:Q, :] = k_scr[j, NB * Q:(NB + 1) * Q, :]
        for j in range(2 * SW_KV_HEADS):
            v_scr[j, 0:Q, :] = v_scr[j, NB * Q:(NB + 1) * Q, :]

    ones_lo = jnp.where(lane == half, 1.0, 0.0)
    ones_hi = jnp.where(lane == 0, 1.0, 0.0)
    for blk in range(NB):
        r0 = blk * Q
        dst = slice(Q + r0, 2 * Q + r0)
        kv = kv_ref[r0:r0 + Q, :].astype(F32)
        kr = rope(kv[:, :LANES], r0)
        kr_sw = pltpu.roll(kr, half, 1)
        vt = kv[:, LANES:]
        vt_sw = pltpu.roll(vt, half, 1)
        k_scr[0, dst, :] = jnp.where(low, kr, kr_sw).astype(BF16)
        k_scr[1, dst, :] = jnp.where(low, kr_sw, kr).astype(BF16)
        v_scr[0, dst, :] = jnp.where(low, vt, ones_lo).astype(BF16)
        v_scr[1, dst, :] = jnp.where(low, ones_hi, vt_sw).astype(BF16)
        v_scr[2, dst, :] = jnp.where(low, vt_sw, ones_lo).astype(BF16)
        v_scr[3, dst, :] = jnp.where(low, ones_hi, vt).astype(BF16)

    qi = lax.broadcasted_iota(jnp.int32, (Q, 2 * Q), 0)
    ki = lax.broadcasted_iota(jnp.int32, (Q, 2 * Q), 1)
    band = (ki > qi) & (ki <= qi + Q)
    first = band & ((ki >= Q) | jnp.logical_not(first_of_seq))

    pairs_per_kv = SW_Q_HEADS // SW_KV_HEADS // 2
    units = []
    for blk in range(NB):
        for pair in range(SW_Q_HEADS // 2):
            units += [(blk, pair, pair // pairs_per_kv, False), (blk, pair, pair // pairs_per_kv, True)]
    qp = {(blk, pair): rope(q_ref[blk * Q:(blk + 1) * Q, pair * LANES:(pair + 1) * LANES].astype(F32),
                            blk * Q) * (SW_HEAD_DIM ** -0.5 * log2e)
          for blk in range(NB) for pair in range(SW_Q_HEADS // 2)}
    s = [_bdot_nt(jnp.where(low, 0.0, qp[blk, pair]) if hi else jnp.where(low, qp[blk, pair], 0.0),
                  k_scr[j, blk * Q:(blk + 2) * Q, :]) for blk, pair, j, hi in units]
    sink = [sink_ref[2 * pair + int(hi)] * log2e for blk, pair, j, hi in units]
    s = [jnp.where(first if units[i][0] == 0 else band, s[i], -jnp.inf) for i in range(len(units))]
    m = [jnp.maximum(jnp.max(s[i], axis=-1, keepdims=True), sink[i]) for i in range(len(units))]
    p = [jnp.exp2(s[i] - m[i]) for i in range(len(units))]
    pv = [_bdot(p[i], v_scr[2 * j + int(hi), blk * Q:(blk + 2) * Q, :])
          for i, (blk, pair, j, hi) in enumerate(units)]
    out = []
    for i, (blk, pair, j, hi) in enumerate(units):
        sum_lane = 0 if hi else half
        denom = pv[i][:, sum_lane:sum_lane + 1] + jnp.exp2(sink[i] - m[i])
        out.append(pv[i] * (1.0 / denom))
    for i in range(0, len(units), 2):
        blk, pair = units[i][0], units[i][1]
        o_ref[blk * Q:(blk + 1) * Q, pair * LANES:(pair + 1) * LANES] = jnp.where(
            low, out[i], out[i + 1]).astype(o_ref.dtype)


def _swa(sw, sinks, cos_t, sin1_t, sin2_t, batch, seq):
    t = sw.shape[0]
    Q = SW_BLOCK
    R = SW_STEP_BLOCKS * Q
    blk = lambda i, s: (i, 0)
    kv_col = SW_Q_W // (2 * SW_KV_W)
    return pl.pallas_call(
        functools.partial(_swa_kernel, steps_per_seq=seq // R),
        grid_spec=pltpu.PrefetchScalarGridSpec(
            num_scalar_prefetch=1,
            grid=(t // R,),
            in_specs=[pl.BlockSpec((R, SW_Q_W), blk),
                      pl.BlockSpec((R, 2 * SW_KV_W), lambda i, s: (i, kv_col)),
                      pl.BlockSpec((R, LANES), blk), pl.BlockSpec((R, LANES), blk),
                      pl.BlockSpec((R, LANES), blk)],
            out_specs=pl.BlockSpec((R, SW_Q_W), blk),
            scratch_shapes=[pltpu.VMEM((SW_KV_HEADS, Q + R, LANES), BF16),
                            pltpu.VMEM((2 * SW_KV_HEADS, Q + R, LANES), BF16)]),
        out_shape=jax.ShapeDtypeStruct((t, SW_Q_W), BF16),
        compiler_params=pltpu.CompilerParams(dimension_semantics=("arbitrary",),
                                             vmem_limit_bytes=VMEM_LIMIT),
        name="swa",
    )(sinks, sw, sw, cos_t, sin1_t, sin2_t)


def _merge_kernel(x_ref, odn_ref, osw_ref, gate_ref, wdn_ref, wsw_ref, wo_ref, g_ref, out_ref):
    tm = x_ref.shape[0]
    subs = [slice(r, r + MERGE_SUB) for r in range(0, tm, MERGE_SUB)]

    def up(rows):
        return (jnp.dot(odn_ref[rows, :], wdn_ref[...], preferred_element_type=F32),
                jnp.dot(osw_ref[rows, :], wsw_ref[...], preferred_element_type=F32))

    def finish(rows, y_a, y_b):
        ga = gate_ref[rows, :D_MODEL].astype(F32)
        gb = gate_ref[rows, D_MODEL:].astype(F32)
        mix = _sigmoid(ga) * y_a + _sigmoid(gb) * y_b
        y = jnp.dot(mix.astype(BF16), wo_ref[...], preferred_element_type=F32)
        out_ref[rows, :] = x_ref[rows, :] + _rms(y, g_ref[...])

    pending = up(subs[0])
    for i, rows in enumerate(subs):
        nxt = up(subs[i + 1]) if i + 1 < len(subs) else None
        finish(rows, *pending)
        pending = nxt


def _merge(x, o_dn, o_sw, gates, w_dn, w_sw, w_o, layer, g, tm=1024):
    t = x.shape[0]
    row = lambda i: (i, 0)
    const = lambda i: (0, 0)
    wspec = pl.BlockSpec((None, D_MODEL, D_MODEL), lambda i: (layer, 0, 0))
    return pl.pallas_call(
        _merge_kernel,
        grid=(t // tm,),
        in_specs=[pl.BlockSpec((tm, D_MODEL), row), pl.BlockSpec((tm, DN_W), row),
                  pl.BlockSpec((tm, SW_Q_W), row), pl.BlockSpec((tm, GATE_COLS), row),
                  wspec, wspec, wspec, pl.BlockSpec((1, D_MODEL), const)],
        out_specs=pl.BlockSpec((tm, D_MODEL), row),
        out_shape=jax.ShapeDtypeStruct((t, D_MODEL), F32),
        compiler_params=pltpu.CompilerParams(dimension_semantics=("arbitrary",),
                                             vmem_limit_bytes=VMEM_LIMIT),
        name="merge",
    )(x, o_dn, o_sw, gates, w_dn, w_sw, w_o, g)


def _mlp_kernel(x_ref, gpre_ref, w1_ref, w2_ref, gpost_ref, out_ref):
    tm = x_ref.shape[0]
    subs = [slice(r, r + MLP_SUB) for r in range(0, tm, MLP_SUB)]

    def hidden(rows):
        return _rms(x_ref[rows, :], gpre_ref[...]).astype(BF16)

    def finish(rows, acc):
        out_ref[rows, :] = x_ref[rows, :] + _rms(acc, gpost_ref[...])

    h = hidden(subs[0])
    done = None
    for i, rows in enumerate(subs):
        acc = None
        h_next = None
        for f in range(0, D_FF, MLP_FF_CHUNK):
            a = jnp.maximum(jnp.dot(h, w1_ref[:, f:f + MLP_FF_CHUNK],
                                    preferred_element_type=F32), 0.0)
            t = jnp.dot((a * a).astype(BF16), w2_ref[f:f + MLP_FF_CHUNK, :],
                        preferred_element_type=F32)
            acc = t if acc is None else acc + t
            if f == 0:
                if i + 1 < len(subs):
                    h_next = hidden(subs[i + 1])
                if done is not None:
                    finish(*done)
        done = (rows, acc)
        h = h_next
    finish(*done)


def _mlp(x, g_pre, w1, w2, layer, g_post, tm=1024):
    t = x.shape[0]
    row = lambda i: (i, 0)
    const = lambda i: (0, 0)
    return pl.pallas_call(
        _mlp_kernel,
        grid=(t // tm,),
        in_specs=[pl.BlockSpec((tm, D_MODEL), row), pl.BlockSpec((1, D_MODEL), const),
                  pl.BlockSpec((None, D_MODEL, D_FF), lambda i: (layer, 0, 0),
                               pipeline_mode=pl.Buffered(1)),
                  pl.BlockSpec((None, D_FF, D_MODEL), lambda i: (layer, 0, 0),
                               pipeline_mode=pl.Buffered(1)),
                  pl.BlockSpec((1, D_MODEL), const)],
        out_specs=pl.BlockSpec((tm, D_MODEL), row),
        out_shape=jax.ShapeDtypeStruct((t, D_MODEL), F32),
        compiler_params=pltpu.CompilerParams(dimension_semantics=("arbitrary",),
                                             vmem_limit_bytes=VMEM_LIMIT),
        name="mlp",
    )(x, g_pre, w1, w2, g_post)


def _pack_w_in(w_in):
    o = 0
    dn = w_in[..., o:o + DN_COLS]
    o += DN_COLS
    ba = w_in[..., o:o + 2 * DN_HEADS]
    o += 2 * DN_HEADS
    sw = w_in[..., o:o + SW_COLS]
    o += SW_COLS
    gates = w_in[..., o:o + GATE_COLS]
    ba = jnp.pad(ba, [(0, 0)] * (ba.ndim - 1) + [(0, BA_COLS - 2 * DN_HEADS)])
    return jnp.concatenate([dn, sw, gates, ba], axis=-1).astype(BF16)


def _lane_row(vec, offset):
    return jnp.zeros((1, LANES), F32).at[0, offset:offset + vec.shape[0]].set(vec.astype(F32))


def kernel(x, positions, pre_mix_g, w_in, dn_conv_w, dn_a_log, dn_dt_bias, dn_norm_g, sw_sinks,
           w_up_dn, w_up_sw, w_o, post_mix_g, pre_mlp_g, w_ff1, w_ff2, post_mlp_g):
    batch, seq, _ = x.shape
    depth = w_in.shape[0]
    t = batch * seq
    xf = x.reshape(t, D_MODEL)
    cos_t, sin1_t, sin2_t = _rope_tables(positions)
    w_in_p = _pack_w_in(w_in)
    w_dn, w_sw, w_out = w_up_dn.astype(BF16), w_up_sw.astype(BF16), w_o.astype(BF16)
    w1, w2 = w_ff1.astype(BF16), w_ff2.astype(BF16)
    for l in range(depth):
        qkv, z, sw, gates, ba = _inproj(xf, pre_mix_g[l].reshape(1, D_MODEL), w_in_p, l,
                                        dn_conv_w[l], seq)
        o_dn = _deltanet(qkv, z, ba, _lane_row(dn_a_log[l], DN_HEADS),
                         _lane_row(dn_dt_bias[l], DN_HEADS), dn_norm_g[l].reshape(1, DN_DV),
                         batch, seq)
        o_sw = _swa(sw, sw_sinks[l].astype(F32), cos_t, sin1_t, sin2_t, batch, seq)
        xf = _merge(xf, o_dn, o_sw, gates, w_dn, w_sw, w_out, l,
                    post_mix_g[l].reshape(1, D_MODEL))
        xf = _mlp(xf, pre_mlp_g[l].reshape(1, D_MODEL), w1, w2, l,
                  post_mlp_g[l].reshape(1, D_MODEL))
    return xf.reshape(batch, seq, D_MODEL)
```
